```python
import math
import jax, jax.numpy as jnp
from jax import lax
import numpy as np

D_MODEL = 1024
BATCH = 8
SEQ = 8192
DEPTH = 1

N_MEM = 256
EPS = 1e-6
CONV_WIDTH = D_MODEL // 2
CONV_K = 3
SSM_WIDTH = D_MODEL // 2
SSM_GROUP = 16
SSM_GROUPS = SSM_WIDTH // SSM_GROUP
SSM_STATE = 64
SSM_CHUNK = 128
MEM_HEADS = 4
MEM_HEAD_DIM = 128
MEM_WIDTH = MEM_HEADS * MEM_HEAD_DIM
N_BRANCH = 3
IN_COLS = 3 * CONV_WIDTH + SSM_WIDTH + MEM_WIDTH + N_BRANCH * D_MODEL
FFN_HIDDEN = int(math.ceil(8 * D_MODEL / 3 / 256) * 256)

kernel_name = "hybrid_gated_conv_s5_memxattn_block"


def rms_norm(x, g):
    xf = x.astype(jnp.float32)
    y = xf * lax.rsqrt(jnp.mean(xf * xf, axis=-1, keepdims=True) + EPS)
    return (y * g.astype(jnp.float32)).astype(x.dtype)


def causal_dwconv(v, w):
    s = v.shape[1]
    vp = jnp.pad(v, ((0, 0), (CONV_K - 1, 0), (0, 0)))
    return w[0] * vp[:, 0:s] + w[1] * vp[:, 1:s + 1] + w[2] * vp[:, 2:s + 2]


def _ssm_combine(e1, e2):
    a1, b1 = e1
    a2, b2 = e2
    return a1 * a2, a2 * b1 + b2


def s5_scan(u, A_re, A_im, log_dt, B_re, B_im, C_re, C_im, D_skip):
    b, s, _ = u.shape
    n_chunks = s // SSM_CHUNK
    f32 = jnp.float32
    uf = u.astype(f32).reshape(b, n_chunks, SSM_CHUNK, SSM_GROUPS, SSM_GROUP)
    uf = uf.transpose(1, 0, 2, 3, 4)
    lam = lax.complex(A_re.astype(f32), A_im.astype(f32))
    dt = jnp.exp(log_dt.astype(f32))
    lam_bar = jnp.exp(lam * dt[:, None])
    Bc = lax.complex(B_re.astype(f32), B_im.astype(f32))
    B_bar = ((lam_bar - 1.0) / lam)[..., None] * Bc
    Cc = lax.complex(C_re.astype(f32), C_im.astype(f32))
    Dv = D_skip.astype(f32).reshape(SSM_GROUPS, SSM_GROUP)

    def chunk_step(h_prev, u_c):
        bu = jnp.einsum('gph,blgh->blgp', B_bar, u_c.astype(jnp.complex64))
        a = jnp.broadcast_to(lam_bar, bu.shape)
        a_cum, h_loc = lax.associative_scan(_ssm_combine, (a, bu), axis=1)
        h = h_loc + a_cum * h_prev[:, None]
        y = jnp.einsum('ghp,blgp->blgh', Cc, h).real + Dv * u_c
        return h[:, -1], y

    h0 = jnp.zeros((b, SSM_GROUPS, SSM_STATE), jnp.complex64)
    _, ys = lax.scan(chunk_step, h0, uf)
    return ys.transpose(1, 0, 2, 3, 4).reshape(b, s, SSM_WIDTH).astype(u.dtype)


def mem_cross_attention(q, mem_n, w_k, w_v):
    b, s, _ = q.shape
    qh = q.reshape(b, s, MEM_HEADS, MEM_HEAD_DIM)
    k = (mem_n @ w_k).reshape(b, N_MEM, MEM_HEADS, MEM_HEAD_DIM)
    v = (mem_n @ w_v).reshape(b, N_MEM, MEM_HEADS, MEM_HEAD_DIM)
    scores = jnp.einsum('bshd,bmhd->bhsm', qh.astype(jnp.float32), k.astype(jnp.float32))
    p = jax.nn.softmax(scores * (MEM_HEAD_DIM ** -0.5), axis=-1).astype(q.dtype)
    o = jnp.einsum('bhsm,bmhd->bshd', p, v)
    return o.reshape(b, s, MEM_WIDTH)


def setup_inputs(seed: int = 0) -> dict:
    key = jax.random.key(seed)
    ks = jax.random.split(key, 32)
    f32 = jnp.float32
    L = DEPTH

    def nrm(k, shape, fan_in):
        return jax.random.normal(k, shape, f32) * (fan_in ** -0.5)

    def gain(k, shape):
        return 1.0 + 0.02 * jax.random.normal(k, shape, f32)

    n_idx = jnp.arange(SSM_STATE, dtype=f32)
    A_re = -0.5 * (1.0 + 0.01 * jax.random.normal(ks[4], (L, SSM_GROUPS, SSM_STATE), f32))
    A_im = math.pi * n_idx * (1.0 + 0.01 * jax.random.normal(ks[5], (L, SSM_GROUPS, SSM_STATE), f32))
    log_dt = jax.random.uniform(ks[6], (L, SSM_GROUPS), f32, math.log(1e-3), math.log(1e-1))
    return {
        "x": jax.random.normal(ks[0], (BATCH, SEQ, D_MODEL), f32),
        "mem": jax.random.normal(ks[1], (BATCH, N_MEM, D_MODEL), f32),
        "norm1_g": gain(ks[2], (L, D_MODEL)),
        "w_in": nrm(ks[3], (L, D_MODEL, IN_COLS), D_MODEL),
        "conv_w": nrm(ks[7], (L, CONV_K, CONV_WIDTH), CONV_K),
        "conv_proj": nrm(ks[8], (L, CONV_WIDTH, D_MODEL), CONV_WIDTH),
        "ssm_A_re": A_re,
        "ssm_A_im": A_im,
        "ssm_log_dt": log_dt,
        "ssm_B_re": nrm(ks[9], (L, SSM_GROUPS, SSM_STATE, SSM_GROUP), 2 * SSM_GROUP),
        "ssm_B_im": nrm(ks[10], (L, SSM_GROUPS, SSM_STATE, SSM_GROUP), 2 * SSM_GROUP),
        "ssm_C_re": nrm(ks[11], (L, SSM_GROUPS, SSM_GROUP, SSM_STATE), SSM_STATE),
        "ssm_C_im": nrm(ks[12], (L, SSM_GROUPS, SSM_GROUP, SSM_STATE), SSM_STATE),
        "ssm_D": jax.random.normal(ks[13], (L, SSM_WIDTH), f32),
        "ssm_glu_w": nrm(ks[14], (L, SSM_WIDTH, SSM_WIDTH), SSM_WIDTH),
        "ssm_glu_b": 0.01 * jax.random.normal(ks[15], (L, SSM_WIDTH), f32),
        "ssm_proj": nrm(ks[16], (L, SSM_WIDTH, D_MODEL), SSM_WIDTH),
        "mem_norm_g": gain(ks[17], (L, D_MODEL)),
        "attn_wk": nrm(ks[18], (L, D_MODEL, MEM_WIDTH), D_MODEL),
        "attn_wv": nrm(ks[19], (L, D_MODEL, MEM_WIDTH), D_MODEL),
        "attn_proj": nrm(ks[20], (L, MEM_WIDTH, D_MODEL), MEM_WIDTH),
        "w_o": nrm(ks[21], (L, D_MODEL, D_MODEL), D_MODEL),
        "norm2_g": gain(ks[22], (L, D_MODEL)),
        "ffn_w_gate": nrm(ks[23], (L, D_MODEL, FFN_HIDDEN), D_MODEL),
        "ffn_w_up": nrm(ks[24], (L, D_MODEL, FFN_HIDDEN), D_MODEL),
        "ffn_w_down": nrm(ks[25], (L, FFN_HIDDEN, D_MODEL), FFN_HIDDEN),
        "final_norm_g": gain(ks[26], (D_MODEL,)),
    }


def reference(x, mem, norm1_g, w_in, conv_w, conv_proj, ssm_A_re, ssm_A_im, ssm_log_dt,
              ssm_B_re, ssm_B_im, ssm_C_re, ssm_C_im, ssm_D, ssm_glu_w, ssm_glu_b, ssm_proj,
              mem_norm_g, attn_wk, attn_wv, attn_proj, w_o, norm2_g, ffn_w_gate, ffn_w_up,
              ffn_w_down, final_norm_g):
    split_at = [CONV_WIDTH, 2 * CONV_WIDTH, 3 * CONV_WIDTH,
                3 * CONV_WIDTH + SSM_WIDTH,
                3 * CONV_WIDTH + SSM_WIDTH + MEM_WIDTH,
                3 * CONV_WIDTH + SSM_WIDTH + MEM_WIDTH + D_MODEL,
                3 * CONV_WIDTH + SSM_WIDTH + MEM_WIDTH + 2 * D_MODEL]
    for l in range(DEPTH):
        h = rms_norm(x, norm1_g[l])
        proj = h @ w_in[l]
        b_gate, c_gate, v, u, q, g_c, g_s, g_m = jnp.split(proj, split_at, axis=-1)

        y_conv = (b_gate * causal_dwconv(c_gate * v, conv_w[l])) @ conv_proj[l]

        y_s = jax.nn.gelu(s5_scan(u, ssm_A_re[l], ssm_A_im[l], ssm_log_dt[l], ssm_B_re[l],
                                  ssm_B_im[l], ssm_C_re[l], ssm_C_im[l], ssm_D[l]))
        y_s = y_s * jax.nn.sigmoid(y_s @ ssm_glu_w[l] + ssm_glu_b[l])
        y_ssm = y_s @ ssm_proj[l]

        mem_n = rms_norm(mem, mem_norm_g[l])
        y_mem = mem_cross_attention(q, mem_n, attn_wk[l], attn_wv[l]) @ attn_proj[l]

        merged = (jax.nn.sigmoid(g_c) * y_conv + jax.nn.sigmoid(g_s) * y_ssm
                  + jax.nn.sigmoid(g_m) * y_mem)
        x = x + merged @ w_o[l]

        h2 = rms_norm(x, norm2_g[l])
        x = x + (jax.nn.silu(h2 @ ffn_w_gate[l]) * (h2 @ ffn_w_up[l])) @ ffn_w_down[l]
    return rms_norm(x, final_norm_g)
```

```python
import functools
import math

import jax
import jax.numpy as jnp
from jax import lax
from jax.experimental import pallas as pl
from jax.experimental.pallas import tpu as pltpu

F32 = jnp.float32
BF16 = jnp.bfloat16

EPS = 1e-6
CONV_K = 3
SSM_GROUP = 16
SSM_STATE = 64
MEM_HEADS = 4
MEM_HEAD_DIM = 128

MXU_TILE = 256
SUBLANES = 8
VMEM_LIMIT_BYTES = 56 * 1024 * 1024

ROW_TILE = 512
TIME_TILE = 64
SCAN_COLS = 512
SCAN_UNROLL = 8


def _rms(x, g):
    ms = jnp.mean(x * x, axis=-1, keepdims=True)
    return x * lax.rsqrt(ms + EPS) * g


def _dot(a, b):
    return jnp.dot(a, b, preferred_element_type=F32)


def _const_spec(shape):
    nd = len(shape)
    return pl.BlockSpec(shape, lambda *_: (0,) * nd, pipeline_mode=pl.Buffered(1))


def _ssm_prep_kernel(are_ref, aim_ref, ldt_ref, bre_ref, bim_ref,
                     lre_ref, lim_ref, bbre_ref, bbim_ref):
    a_re = are_ref[...]
    a_im = aim_ref[...]
    dt = jnp.exp(ldt_ref[...])
    mag = jnp.exp(a_re * dt)
    l_re = mag * jnp.cos(a_im * dt)
    l_im = mag * jnp.sin(a_im * dt)
    lre_ref[...] = l_re
    lim_ref[...] = l_im
    n_re = l_re - 1.0
    n_im = l_im
    den = a_re * a_re + a_im * a_im
    c_re = (n_re * a_re + n_im * a_im) / den
    c_im = (n_im * a_re - n_re * a_im) / den
    b_re = bre_ref[...]
    b_im = bim_ref[...]
    c_re = c_re[:, None, :]
    c_im = c_im[:, None, :]
    bbre_ref[...] = c_re * b_re - c_im * b_im
    bbim_ref[...] = c_re * b_im + c_im * b_re


def _ssm_prep(a_re, a_im, log_dt, b_re_t, b_im_t):
    g, p = a_re.shape
    h = b_re_t.shape[1]
    return pl.pallas_call(
        _ssm_prep_kernel,
        out_shape=(jax.ShapeDtypeStruct((g, p), F32), jax.ShapeDtypeStruct((g, p), F32),
                   jax.ShapeDtypeStruct((g, h, p), F32), jax.ShapeDtypeStruct((g, h, p), F32)),
        name="ssm_prep",
    )(a_re, a_im, log_dt.reshape(g, 1), b_re_t, b_im_t)


def _mem_kv_kernel(mem_ref, g_ref, wkt_ref, wv_ref, kt_ref, v_ref):
    mn = _rms(mem_ref[...], g_ref[...]).astype(BF16)
    kt = lax.dot_general(wkt_ref[...], mn, (((1,), (1,)), ((), ())),
                         preferred_element_type=F32)
    kt_ref[...] = kt.astype(BF16)
    v_ref[...] = _dot(mn, wv_ref[...]).astype(BF16)


def _mem_kv(mem, g, wkt, wv):
    b, m, d = mem.shape
    w = wv.shape[1]
    return pl.pallas_call(
        _mem_kv_kernel,
        grid=(b,),
        in_specs=[pl.BlockSpec((None, m, d), lambda i: (i, 0, 0)),
                  pl.BlockSpec((1, d), lambda i: (0, 0)),
                  pl.BlockSpec((w, d), lambda i: (0, 0)),
                  pl.BlockSpec((d, w), lambda i: (0, 0))],
        out_specs=(pl.BlockSpec((None, w, m), lambda i: (i, 0, 0)),
                   pl.BlockSpec((None, m, w), lambda i: (i, 0, 0))),
        out_shape=(jax.ShapeDtypeStruct((b, w, m), BF16), jax.ShapeDtypeStruct((b, m, w), BF16)),
        name="mem_kv",
    )(mem, g, wkt, wv)


def _mixer_kernel(x_ref, g1_ref, win_ref, convw_ref, convp_ref, kt_ref, v_ref, attnp_ref,
                  u_ref, sgs_ref, part_ref, cv_ref, *, cw, sw, mw, d):
    i = pl.program_id(1)
    tb = x_ref.shape[0]
    h = _rms(x_ref[...], g1_ref[...]).astype(BF16)

    def proj(lo, hi):
        return _dot(h, win_ref[:, lo:hi])

    o_u = 3 * cw
    o_q = o_u + sw
    o_gc = o_q + mw
    o_gs = o_gc + d
    o_gm = o_gs + d

    cv = proj(cw, 2 * cw) * proj(2 * cw, 3 * cw)

    @pl.when(i == 0)
    def _():
        cv_ref[0:SUBLANES, :] = jnp.zeros((SUBLANES, cw), F32)

    cv_ref[SUBLANES:SUBLANES + tb, :] = cv
    conv = (convw_ref[0:1, :] * cv_ref[SUBLANES - 2:SUBLANES - 2 + tb, :]
            + convw_ref[1:2, :] * cv_ref[SUBLANES - 1:SUBLANES - 1 + tb, :]
            + convw_ref[2:3, :] * cv)
    cv_ref[0:SUBLANES, :] = cv_ref[tb:tb + SUBLANES, :]
    pre = (proj(0, cw) * conv).astype(BF16)
    part = jax.nn.sigmoid(proj(o_gc, o_gs)) * _dot(pre, convp_ref[...])

    q = proj(o_q, o_gc)
    scale = MEM_HEAD_DIM ** -0.5
    heads = []
    for hd in range(MEM_HEADS):
        lo = hd * MEM_HEAD_DIM
        hi = lo + MEM_HEAD_DIM
        s = _dot(q[:, lo:hi].astype(BF16), kt_ref[lo:hi, :]) * scale
        e = jnp.exp(s - jnp.max(s, axis=-1, keepdims=True))
        p = e / jnp.sum(e, axis=-1, keepdims=True)
        heads.append(_dot(p.astype(BF16), v_ref[:, lo:hi]))
    o = jnp.concatenate(heads, axis=-1).astype(BF16)
    part = part + jax.nn.sigmoid(proj(o_gm, o_gm + d)) * _dot(o, attnp_ref[...])

    part_ref[...] = part.astype(BF16)
    sgs_ref[...] = jax.nn.sigmoid(proj(o_gs, o_gm)).astype(BF16)
    u_ref[...] = proj(o_u, o_q).astype(BF16)


def _mixer(x, g1, w_in, conv_w, conv_p, kt, v, attn_p, *, cw, sw, mw):
    b, s, d = x.shape
    tb = ROW_TILE
    m = kt.shape[2]
    kern = functools.partial(_mixer_kernel, cw=cw, sw=sw, mw=mw, d=d)
    return pl.pallas_call(
        kern,
        grid=(b, s // tb),
        in_specs=[pl.BlockSpec((None, tb, d), lambda bi, i: (bi, i, 0)),
                  _const_spec((1, d)),
                  _const_spec(w_in.shape),
                  _const_spec(conv_w.shape),
                  _const_spec(conv_p.shape),
                  pl.BlockSpec((None, mw, m), lambda bi, i: (bi, 0, 0)),
                  pl.BlockSpec((None, m, mw), lambda bi, i: (bi, 0, 0)),
                  _const_spec(attn_p.shape)],
        out_specs=(pl.BlockSpec((tb, sw), lambda bi, i: (i, bi)),
                   pl.BlockSpec((None, tb, d), lambda bi, i: (bi, i, 0)),
                   pl.BlockSpec((None, tb, d), lambda bi, i: (bi, i, 0))),
        out_shape=(jax.ShapeDtypeStruct((s, b * sw), BF16),
                   jax.ShapeDtypeStruct((b, s, d), BF16),
                   jax.ShapeDtypeStruct((b, s, d), BF16)),
        scratch_shapes=[pltpu.VMEM((tb + SUBLANES, cw), F32)],
        compiler_params=pltpu.CompilerParams(
            dimension_semantics=("arbitrary", "arbitrary"),
            vmem_limit_bytes=VMEM_LIMIT_BYTES),
        name="mixer",
    )(x, g1, w_in, conv_w, conv_p, kt, v, attn_p)


def _s5_kernel(u_ref, bc_ref, lre_ref, lim_ref, cr_ref, ci_ref, dsk_ref, gw_ref, gb_ref, sp_ref,
               y_ref, bu_ref, hst_ref, *, n_state, batch):
    rows, sw = u_ref.shape
    n_t = rows // batch
    n_tiles = 2 * n_state // MXU_TILE
    half_k = sw // 2

    @pl.when(pl.program_id(0) == 0)
    def _():
        hst_ref[...] = jnp.zeros_like(hst_ref)

    u = u_ref[...]
    for j in range(n_tiles):
        gq = j % (n_tiles // 2)
        kh = gq // (n_tiles // 4)
        bu_ref[:, j * MXU_TILE:(j + 1) * MXU_TILE] = _dot(
            u[:, kh * half_k:(kh + 1) * half_k], bc_ref[j])

    for c in range(n_state // SCAN_COLS):
        re = slice(c * SCAN_COLS, (c + 1) * SCAN_COLS)
        im = slice(n_state + c * SCAN_COLS, n_state + (c + 1) * SCAN_COLS)
        l_re = lre_ref[:, re]
        l_im = lim_ref[:, re]

        def step(t, carry):
            h_re, h_im = carry
            r0 = pl.multiple_of(t * batch, batch)
            n_re = l_re * h_re - l_im * h_im + bu_ref[pl.ds(r0, batch), re]
            n_im = l_re * h_im + l_im * h_re + bu_ref[pl.ds(r0, batch), im]
            bu_ref[pl.ds(r0, batch), re] = n_re
            bu_ref[pl.ds(r0, batch), im] = n_im
            return n_re, n_im

        h_re, h_im = lax.fori_loop(0, n_t, step, (hst_ref[:, re], hst_ref[:, im]),
                                   unroll=SCAN_UNROLL)
        hst_ref[:, re] = h_re
        hst_ref[:, im] = h_im

    half_s = n_state // 2
    ys = []
    for nh in range(2):
        h_re = bu_ref[:, nh * half_s:(nh + 1) * half_s].astype(BF16)
        h_im = bu_ref[:, n_state + nh * half_s:n_state + (nh + 1) * half_s].astype(BF16)
        ys.append(_dot(h_re, cr_ref[nh]) + _dot(h_im, ci_ref[nh]))
    y = jnp.concatenate(ys, axis=-1) + dsk_ref[...] * u.astype(F32)

    y = jax.nn.gelu(y)
    y = y * jax.nn.sigmoid(_dot(y.astype(BF16), gw_ref[...]) + gb_ref[...])
    y_ref[...] = _dot(y.astype(BF16), sp_ref[...]).astype(BF16)


def _s5(u_tm, bc, lre, lim, cr, ci, dsk, glu_w, glu_b, ssm_p, *, batch):
    rows_total, sw = u_tm.shape
    d = ssm_p.shape[1]
    n_state = lre.shape[1]
    rows = TIME_TILE * batch
    kern = functools.partial(_s5_kernel, n_state=n_state, batch=batch)
    return pl.pallas_call(
        kern,
        grid=(rows_total // rows,),
        in_specs=[pl.BlockSpec((rows, sw), lambda i: (i, 0)),
                  _const_spec(bc.shape), _const_spec(lre.shape), _const_spec(lim.shape),
                  _const_spec(cr.shape), _const_spec(ci.shape), _const_spec(dsk.shape),
                  _const_spec(glu_w.shape), _const_spec(glu_b.shape), _const_spec(ssm_p.shape)],
        out_specs=pl.BlockSpec((rows, d), lambda i: (i, 0)),
        out_shape=jax.ShapeDtypeStruct((rows_total, d), BF16),
        scratch_shapes=[pltpu.VMEM((rows, 2 * n_state), F32),
                        pltpu.VMEM((batch, 2 * n_state), F32)],
        compiler_params=pltpu.CompilerParams(
            dimension_semantics=("arbitrary",),
            vmem_limit_bytes=VMEM_LIMIT_BYTES),
        name="s5",
    )(u_tm, bc, lre, lim, cr, ci, dsk, glu_w, glu_b, ssm_p)


def _ffn_kernel(x_ref, part_ref, sgs_ref, yssm_ref, wo_ref, g2_ref, wg_ref, wu_ref, wd_ref,
                gf_ref, o_ref, *, chunks):
    merged = part_ref[...].astype(F32) + sgs_ref[...].astype(F32) * yssm_ref[...].astype(F32)
    x1 = x_ref[...] + _dot(merged.astype(BF16), wo_ref[...])
    h2 = _rms(x1, g2_ref[...]).astype(BF16)
    acc = x1
    for lo, hi in chunks:
        g = _dot(h2, wg_ref[:, lo:hi])
        a = (g * jax.nn.sigmoid(g) * _dot(h2, wu_ref[:, lo:hi])).astype(BF16)
        acc = acc + _dot(a, wd_ref[lo:hi, :])
    o_ref[...] = _rms(acc, gf_ref[...])


def _ffn_chunks(hidden):
    step = 4 * MXU_TILE
    return tuple((lo, min(lo + step, hidden)) for lo in range(0, hidden, step))


def _ffn(x, part, sgs, yssm_tm, w_o, g2, wg, wu, wd, gf):
    b, s, d = x.shape
    tb = ROW_TILE
    kern = functools.partial(_ffn_kernel, chunks=_ffn_chunks(wg.shape[1]))
    row_spec = pl.BlockSpec((None, tb, d), lambda bi, i: (bi, i, 0))
    return pl.pallas_call(
        kern,
        grid=(b, s // tb),
        in_specs=[row_spec, row_spec, row_spec,
                  pl.BlockSpec((tb, d), lambda bi, i: (i, bi)),
                  _const_spec(w_o.shape), _const_spec((1, d)),
                  _const_spec(wg.shape), _const_spec(wu.shape), _const_spec(wd.shape),
                  _const_spec((1, d))],
        out_specs=row_spec,
        out_shape=jax.ShapeDtypeStruct((b, s, d), x.dtype),
        compiler_params=pltpu.CompilerParams(
            dimension_semantics=("arbitrary", "arbitrary"),
            vmem_limit_bytes=VMEM_LIMIT_BYTES),
        name="ffn",
    )(x, part, sgs, yssm_tm, w_o, g2, wg, wu, wd, gf)


def _pack_b(bb_re, bb_im):
    g, h, p = bb_re.shape
    per_tile = MXU_TILE // p
    per_half = MXU_TILE // h
    n_q = g // per_tile
    bb = jnp.stack([bb_re, bb_im]).reshape(2, n_q, per_tile, h, p)
    q = jnp.arange(n_q)[:, None, None]
    a = jnp.arange(per_tile)[None, :, None]
    l = jnp.arange(per_half)[None, None, :]
    onehot = (l == (q * per_tile + a) % per_half).astype(F32)
    tiles = jnp.einsum('qal,rqahp->rqlhap', onehot, bb)
    return tiles.reshape(2 * n_q, MXU_TILE, MXU_TILE)


def _pack_c(c):
    g, h, p = c.shape
    c4 = c.reshape(2, g // 2, h, p)
    eye = jnp.eye(g // 2, dtype=F32)
    return jnp.einsum('xghp,gk->xgpkh', c4, eye).reshape(2, (g // 2) * p, (g // 2) * h)


def kernel(x, mem, norm1_g, w_in, conv_w, conv_proj, ssm_A_re, ssm_A_im, ssm_log_dt, ssm_B_re,
           ssm_B_im, ssm_C_re, ssm_C_im, ssm_D, ssm_glu_w, ssm_glu_b, ssm_proj, mem_norm_g,
           attn_wk, attn_wv, attn_proj, w_o, norm2_g, ffn_w_gate, ffn_w_up, ffn_w_down,
           final_norm_g):
    b, s, d = x.shape
    depth = w_in.shape[0]
    cw = conv_w.shape[2]
    sw = ssm_D.shape[1]
    mw = attn_wk.shape[2]
    n_groups = ssm_A_re.shape[1]
    n_state = n_groups * SSM_STATE
    assert depth == 1, "the final norm is fused into the single layer's ffn kernel"
    assert b == SUBLANES, "the S5 recurrence keeps the batch on the sublane axis"

    for l in range(depth):
        lre_g, lim_g, bb_re, bb_im = _ssm_prep(
            ssm_A_re[l], ssm_A_im[l], ssm_log_dt[l],
            ssm_B_re[l].transpose(0, 2, 1), ssm_B_im[l].transpose(0, 2, 1))
        bc = _pack_b(bb_re, bb_im).astype(BF16)
        cr = _pack_c(ssm_C_re[l]).astype(BF16)
        ci = _pack_c(-ssm_C_im[l]).astype(BF16)
        lre = jnp.broadcast_to(lre_g.reshape(1, n_state), (b, n_state))
        lim = jnp.broadcast_to(lim_g.reshape(1, n_state), (b, n_state))

        kt, v = _mem_kv(mem, mem_norm_g[l].reshape(1, d), attn_wk[l].T.astype(BF16),
                        attn_wv[l].astype(BF16))
        u_tm, sgs, part = _mixer(
            x, norm1_g[l].reshape(1, d), w_in[l].astype(BF16), conv_w[l],
            conv_proj[l].astype(BF16), kt, v, attn_proj[l].astype(BF16), cw=cw, sw=sw, mw=mw)
        yssm_tm = _s5(u_tm.reshape(s * b, sw), bc, lre, lim, cr, ci, ssm_D[l].reshape(1, sw),
                      ssm_glu_w[l].astype(BF16), ssm_glu_b[l].reshape(1, sw),
                      ssm_proj[l].astype(BF16), batch=b)
        x = _ffn(x, part, sgs, yssm_tm.reshape(s, b * d), w_o[l].astype(BF16),
                 norm2_g[l].reshape(1, d), ffn_w_gate[l].astype(BF16), ffn_w_up[l].astype(BF16),
                 ffn_w_down[l].astype(BF16), final_norm_g.reshape(1, d))
    return x
```

```python
import functools
import math

import jax
import jax.numpy as jnp
from jax import lax
from jax.experimental import pallas as pl
from jax.experimental.pallas import tpu as pltpu

F32 = jnp.float32
BF16 = jnp.bfloat16

EPS = 1e-6
CONV_K = 3
SSM_GROUP = 16
SSM_STATE = 64
MEM_HEADS = 4
MEM_HEAD_DIM = 128

MXU_TILE = 256
SUBLANES = 8
LANES = 128
VMEM_LIMIT_BYTES = 56 * 1024 * 1024

ROW_TILE = 512
TIME_TILE = 64
SCAN_COLS = 512
SCAN_UNROLL = 8


def _rms(x, g):
    ms = jnp.mean(x * x, axis=-1, keepdims=True)
    return x * lax.rsqrt(ms + EPS) * g


def _dot(a, b):
    return jnp.dot(a, b, preferred_element_type=F32)


def _const_spec(shape):
    nd = len(shape)
    return pl.BlockSpec(shape, lambda *_: (0,) * nd, pipeline_mode=pl.Buffered(1))


def _ssm_prep_kernel(are_ref, aim_ref, ldt_ref, bre_ref, bim_ref,
                     lre_ref, lim_ref, bbre_ref, bbim_ref):
    a_re = are_ref[...]
    a_im = aim_ref[...]
    dt = jnp.exp(ldt_ref[...])
    mag = jnp.exp(a_re * dt)
    l_re = mag * jnp.cos(a_im * dt)
    l_im = mag * jnp.sin(a_im * dt)
    lre_ref[...] = l_re
    lim_ref[...] = l_im
    n_re = l_re - 1.0
    n_im = l_im
    den = a_re * a_re + a_im * a_im
    c_re = (n_re * a_re + n_im * a_im) / den
    c_im = (n_im * a_re - n_re * a_im) / den
    b_re = bre_ref[...]
    b_im = bim_ref[...]
    c_re = c_re[:, None, :]
    c_im = c_im[:, None, :]
    bbre_ref[...] = c_re * b_re - c_im * b_im
    bbim_ref[...] = c_re * b_im + c_im * b_re


def _ssm_prep(a_re, a_im, log_dt, b_re_t, b_im_t):
    g, p = a_re.shape
    h = b_re_t.shape[1]
    return pl.pallas_call(
        _ssm_prep_kernel,
        out_shape=(jax.ShapeDtypeStruct((g, p), F32), jax.ShapeDtypeStruct((g, p), F32),
                   jax.ShapeDtypeStruct((g, h, p), F32), jax.ShapeDtypeStruct((g, h, p), F32)),
        name="ssm_prep",
    )(a_re, a_im, log_dt.reshape(g, 1), b_re_t, b_im_t)


def _mem_kv_kernel(mem_ref, g_ref, wkt_ref, wv_ref, kt_ref, v_ref):
    mn = _rms(mem_ref[...], g_ref[...]).astype(BF16)
    kt = lax.dot_general(wkt_ref[...], mn, (((1,), (1,)), ((), ())),
                         preferred_element_type=F32)
    kt_ref[...] = kt.astype(BF16)
    v_ref[...] = _dot(mn, wv_ref[...]).astype(BF16)


def _mem_kv(mem, g, wkt, wv):
    b, m, d = mem.shape
    w = wv.shape[1]
    return pl.pallas_call(
        _mem_kv_kernel,
        grid=(b,),
        in_specs=[pl.BlockSpec((None, m, d), lambda i: (i, 0, 0)),
                  pl.BlockSpec((1, d), lambda i: (0, 0)),
                  pl.BlockSpec((w, d), lambda i: (0, 0)),
                  pl.BlockSpec((d, w), lambda i: (0, 0))],
        out_specs=(pl.BlockSpec((None, w, m), lambda i: (i, 0, 0)),
                   pl.BlockSpec((None, m, w), lambda i: (i, 0, 0))),
        out_shape=(jax.ShapeDtypeStruct((b, w, m), BF16), jax.ShapeDtypeStruct((b, m, w), BF16)),
        name="mem_kv",
    )(mem, g, wkt, wv)


def _mixer_kernel(x_ref, g1_ref, win_ref, convw_ref, convp_ref, kt_ref, v_ref, attnp_ref,
                  u_ref, sgs_ref, part_ref, cv_ref, *, cw, sw, mw, d):
    i = pl.program_id(1)
    tb = x_ref.shape[0]
    h = _rms(x_ref[...], g1_ref[...]).astype(BF16)

    def proj(lo, hi):
        return _dot(h, win_ref[:, lo:hi])

    o_u = 3 * cw
    o_q = o_u + sw
    o_gc = o_q + mw
    o_gs = o_gc + d
    o_gm = o_gs + d

    cv = proj(cw, 2 * cw) * proj(2 * cw, 3 * cw)

    @pl.when(i == 0)
    def _():
        cv_ref[0:SUBLANES, :] = jnp.zeros((SUBLANES, cw), F32)

    cv_ref[SUBLANES:SUBLANES + tb, :] = cv
    conv = (convw_ref[0:1, :] * cv_ref[SUBLANES - 2:SUBLANES - 2 + tb, :]
            + convw_ref[1:2, :] * cv_ref[SUBLANES - 1:SUBLANES - 1 + tb, :]
            + convw_ref[2:3, :] * cv)
    cv_ref[0:SUBLANES, :] = cv_ref[tb:tb + SUBLANES, :]
    pre = (proj(0, cw) * conv).astype(BF16)
    part = jax.nn.sigmoid(proj(o_gc, o_gs)) * _dot(pre, convp_ref[...])

    q = proj(o_q, o_gc)
    scale = MEM_HEAD_DIM ** -0.5
    heads = []
    for hd in range(MEM_HEADS):
        lo = hd * MEM_HEAD_DIM
        hi = lo + MEM_HEAD_DIM
        s = _dot(q[:, lo:hi].astype(BF16), kt_ref[lo:hi, :]) * scale
        e = jnp.exp(s - jnp.max(s, axis=-1, keepdims=True))
        p = e / jnp.sum(e, axis=-1, keepdims=True)
        heads.append(_dot(p.astype(BF16), v_ref[:, lo:hi]))
    o = jnp.concatenate(heads, axis=-1).astype(BF16)
    part = part + jax.nn.sigmoid(proj(o_gm, o_gm + d)) * _dot(o, attnp_ref[...])

    part_ref[...] = part.astype(BF16)
    sgs_ref[...] = jax.nn.sigmoid(proj(o_gs, o_gm)).astype(BF16)
    u_ref[...] = proj(o_u, o_q).astype(BF16)


def _mixer(x, g1, w_in, conv_w, conv_p, kt, v, attn_p, *, cw, sw, mw):
    b, s, d = x.shape
    tb = ROW_TILE
    m = kt.shape[2]
    kern = functools.partial(_mixer_kernel, cw=cw, sw=sw, mw=mw, d=d)
    return pl.pallas_call(
        kern,
        grid=(b, s // tb),
        in_specs=[pl.BlockSpec((None, tb, d), lambda bi, i: (bi, i, 0)),
                  _const_spec((1, d)),
                  _const_spec(w_in.shape),
                  _const_spec(conv_w.shape),
                  _const_spec(conv_p.shape),
                  pl.BlockSpec((None, mw, m), lambda bi, i: (bi, 0, 0)),
                  pl.BlockSpec((None, m, mw), lambda bi, i: (bi, 0, 0)),
                  _const_spec(attn_p.shape)],
        out_specs=(pl.BlockSpec((None, tb, sw), lambda bi, i: (bi, i, 0)),
                   pl.BlockSpec((None, tb, d), lambda bi, i: (bi, i, 0)),
                   pl.BlockSpec((None, tb, d), lambda bi, i: (bi, i, 0))),
        out_shape=(jax.ShapeDtypeStruct((b, s, sw), BF16),
                   jax.ShapeDtypeStruct((b, s, d), BF16),
                   jax.ShapeDtypeStruct((b, s, d), BF16)),
        scratch_shapes=[pltpu.VMEM((tb + SUBLANES, cw), F32)],
        compiler_params=pltpu.CompilerParams(
            dimension_semantics=("arbitrary", "arbitrary"),
            vmem_limit_bytes=VMEM_LIMIT_BYTES),
        name="mixer",
    )(x, g1, w_in, conv_w, conv_p, kt, v, attn_p)


def _s5_kernel(u_ref, bc_ref, lre_ref, lim_ref, cr_ref, ci_ref, dsk_ref, gw_ref, gb_ref, sp_ref,
               y_ref, bu_ref, hst_ref, ubm_ref, utm_ref, ytm_ref, *, n_state):
    batch, n_t, sw = u_ref.shape
    rows = batch * n_t
    n_tiles = 2 * n_state // MXU_TILE
    half_k = sw // 2

    @pl.when(pl.program_id(0) == 0)
    def _():
        hst_ref[...] = jnp.zeros_like(hst_ref)

    ubm = u_ref[...].reshape(rows, sw).astype(F32)
    for c in range(sw // LANES):
        cols = slice(c * LANES, (c + 1) * LANES)
        ubm_ref[c] = ubm[:, cols]
        for t in range(n_t):
            utm_ref[t * batch:(t + 1) * batch, cols] = ubm_ref[c, pl.ds(t, batch, stride=n_t), :]
    u32 = utm_ref[...]
    u = u32.astype(BF16)

    for j in range(n_tiles):
        gq = j % (n_tiles // 2)
        kh = gq // (n_tiles // 4)
        bu_ref[:, j * MXU_TILE:(j + 1) * MXU_TILE] = _dot(
            u[:, kh * half_k:(kh + 1) * half_k], bc_ref[j])

    for c in range(n_state // SCAN_COLS):
        re = slice(c * SCAN_COLS, (c + 1) * SCAN_COLS)
        im = slice(n_state + c * SCAN_COLS, n_state + (c + 1) * SCAN_COLS)
        l_re = lre_ref[:, re]
        l_im = lim_ref[:, re]

        def step(t, carry):
            h_re, h_im = carry
            r0 = pl.multiple_of(t * batch, batch)
            n_re = l_re * h_re - l_im * h_im + bu_ref[pl.ds(r0, batch), re]
            n_im = l_re * h_im + l_im * h_re + bu_ref[pl.ds(r0, batch), im]
            bu_ref[pl.ds(r0, batch), re] = n_re
            bu_ref[pl.ds(r0, batch), im] = n_im
            return n_re, n_im

        h_re, h_im = lax.fori_loop(0, n_t, step, (hst_ref[:, re], hst_ref[:, im]),
                                   unroll=SCAN_UNROLL)
        hst_ref[:, re] = h_re
        hst_ref[:, im] = h_im

    half_s = n_state // 2
    ys = []
    for nh in range(2):
        h_re = bu_ref[:, nh * half_s:(nh + 1) * half_s].astype(BF16)
        h_im = bu_ref[:, n_state + nh * half_s:n_state + (nh + 1) * half_s].astype(BF16)
        ys.append(_dot(h_re, cr_ref[nh]) + _dot(h_im, ci_ref[nh]))
    y = jnp.concatenate(ys, axis=-1) + dsk_ref[...] * u32

    y = jax.nn.gelu(y)
    y = y * jax.nn.sigmoid(_dot(y.astype(BF16), gw_ref[...]) + gb_ref[...])
    ytm = _dot(y.astype(BF16), sp_ref[...])
    for c in range(ytm.shape[1] // LANES):
        cols = slice(c * LANES, (c + 1) * LANES)
        ytm_ref[c] = ytm[:, cols]
        for bb in range(batch):
            y_ref[bb, :, cols] = ytm_ref[c, pl.ds(bb, n_t, stride=batch), :].astype(BF16)


def _s5(u, bc, lre, lim, cr, ci, dsk, glu_w, glu_b, ssm_p):
    batch, s, sw = u.shape
    d = ssm_p.shape[1]
    n_state = lre.shape[1]
    n_t = TIME_TILE
    rows = n_t * batch
    kern = functools.partial(_s5_kernel, n_state=n_state)
    return pl.pallas_call(
        kern,
        grid=(s // n_t,),
        in_specs=[pl.BlockSpec((batch, n_t, sw), lambda i: (0, i, 0)),
                  _const_spec(bc.shape), _const_spec(lre.shape), _const_spec(lim.shape),
                  _const_spec(cr.shape), _const_spec(ci.shape), _const_spec(dsk.shape),
                  _const_spec(glu_w.shape), _const_spec(glu_b.shape), _const_spec(ssm_p.shape)],
        out_specs=pl.BlockSpec((batch, n_t, d), lambda i: (0, i, 0)),
        out_shape=jax.ShapeDtypeStruct((batch, s, d), BF16),
        scratch_shapes=[pltpu.VMEM((rows, 2 * n_state), F32),
                        pltpu.VMEM((batch, 2 * n_state), F32),
                        pltpu.VMEM((sw // LANES, rows, LANES), F32),
                        pltpu.VMEM((rows, sw), F32),
                        pltpu.VMEM((d // LANES, rows, LANES), F32)],
        compiler_params=pltpu.CompilerParams(
            dimension_semantics=("arbitrary",),
            vmem_limit_bytes=VMEM_LIMIT_BYTES),
        name="s5",
    )(u, bc, lre, lim, cr, ci, dsk, glu_w, glu_b, ssm_p)


def _ffn_kernel(x_ref, part_ref, sgs_ref, yssm_ref, wo_ref, g2_ref, wg_ref, wu_ref, wd_ref,
                gf_ref, o_ref, *, chunks):
    merged = part_ref[...].astype(F32) + sgs_ref[...].astype(F32) * yssm_ref[...].astype(F32)
    x1 = x_ref[...] + _dot(merged.astype(BF16), wo_ref[...])
    h2 = _rms(x1, g2_ref[...]).astype(BF16)
    acc = x1
    for lo, hi in chunks:
        g = _dot(h2, wg_ref[:, lo:hi])
        a = (g * jax.nn.sigmoid(g) * _dot(h2, wu_ref[:, lo:hi])).astype(BF16)
        acc = acc + _dot(a, wd_ref[lo:hi, :])
    o_ref[...] = _rms(acc, gf_ref[...])


def _ffn_chunks(hidden):
    step = 4 * MXU_TILE
    return tuple((lo, min(lo + step, hidden)) for lo in range(0, hidden, step))


def _ffn(x, part, sgs, yssm, w_o, g2, wg, wu, wd, gf):
    b, s, d = x.shape
    tb = ROW_TILE
    kern = functools.partial(_ffn_kernel, chunks=_ffn_chunks(wg.shape[1]))
    row_spec = pl.BlockSpec((None, tb, d), lambda bi, i: (bi, i, 0))
    return pl.pallas_call(
        kern,
        grid=(b, s // tb),
        in_specs=[row_spec, row_spec, row_spec, row_spec,
                  _const_spec(w_o.shape), _const_spec((1, d)),
                  _const_spec(wg.shape), _const_spec(wu.shape), _const_spec(wd.shape),
                  _const_spec((1, d))],
        out_specs=row_spec,
        out_shape=jax.ShapeDtypeStruct((b, s, d), x.dtype),
        compiler_params=pltpu.CompilerParams(
            dimension_semantics=("arbitrary", "arbitrary"),
            vmem_limit_bytes=VMEM_LIMIT_BYTES),
        name="ffn",
    )(x, part, sgs, yssm, w_o, g2, wg, wu, wd, gf)


def _pack_b(bb_re, bb_im):
    g, h, p = bb_re.shape
    per_tile = MXU_TILE // p
    per_half = MXU_TILE // h
    n_q = g // per_tile
    bb = jnp.stack([bb_re, bb_im]).reshape(2, n_q, per_tile, h, p)
    q = jnp.arange(n_q)[:, None, None]
    a = jnp.arange(per_tile)[None, :, None]
    l = jnp.arange(per_half)[None, None, :]
    onehot = (l == (q * per_tile + a) % per_half).astype(F32)
    tiles = jnp.einsum('qal,rqahp->rqlhap', onehot, bb)
    return tiles.reshape(2 * n_q, MXU_TILE, MXU_TILE)


def _pack_c(c):
    g, h, p = c.shape
    c4 = c.reshape(2, g // 2, h, p)
    eye = jnp.eye(g // 2, dtype=F32)
    return jnp.einsum('xghp,gk->xgpkh', c4, eye).reshape(2, (g // 2) * p, (g // 2) * h)


def kernel(x, mem, norm1_g, w_in, conv_w, conv_proj, ssm_A_re, ssm_A_im, ssm_log_dt, ssm_B_re,
           ssm_B_im, ssm_C_re, ssm_C_im, ssm_D, ssm_glu_w, ssm_glu_b, ssm_proj, mem_norm_g,
           attn_wk, attn_wv, attn_proj, w_o, norm2_g, ffn_w_gate, ffn_w_up, ffn_w_down,
           final_norm_g):
    b, s, d = x.shape
    depth = w_in.shape[0]
    cw = conv_w.shape[2]
    sw = ssm_D.shape[1]
    mw = attn_wk.shape[2]
    n_groups = ssm_A_re.shape[1]
    n_state = n_groups * SSM_STATE
    assert depth == 1, "the final norm is fused into the single layer's ffn kernel"
    assert b == SUBLANES, "the S5 recurrence keeps the batch on the sublane axis"

    for l in range(depth):
        lre_g, lim_g, bb_re, bb_im = _ssm_prep(
            ssm_A_re[l], ssm_A_im[l], ssm_log_dt[l],
            ssm_B_re[l].transpose(0, 2, 1), ssm_B_im[l].transpose(0, 2, 1))
        bc = _pack_b(bb_re, bb_im).astype(BF16)
        cr = _pack_c(ssm_C_re[l]).astype(BF16)
        ci = _pack_c(-ssm_C_im[l]).astype(BF16)
        lre = jnp.broadcast_to(lre_g.reshape(1, n_state), (b, n_state))
        lim = jnp.broadcast_to(lim_g.reshape(1, n_state), (b, n_state))

        kt, v = _mem_kv(mem, mem_norm_g[l].reshape(1, d), attn_wk[l].T.astype(BF16),
                        attn_wv[l].astype(BF16))
        u, sgs, part = _mixer(
            x, norm1_g[l].reshape(1, d), w_in[l].astype(BF16), conv_w[l],
            conv_proj[l].astype(BF16), kt, v, attn_proj[l].astype(BF16), cw=cw, sw=sw, mw=mw)
        yssm = _s5(u, bc, lre, lim, cr, ci, ssm_D[l].reshape(1, sw),
                   ssm_glu_w[l].astype(BF16), ssm_glu_b[l].reshape(1, sw),
                   ssm_proj[l].astype(BF16))
        x = _ffn(x, part, sgs, yssm, w_o[l].astype(BF16),
                 norm2_g[l].reshape(1, d), ffn_w_gate[l].astype(BF16), ffn_w_up[l].astype(BF16),
                 ffn_w_down[l].astype(BF16), final_norm_g.reshape(1, d))
    return x
```

```python
import functools
import math

import jax
import jax.numpy as jnp
from jax import lax
from jax.experimental import pallas as pl
from jax.experimental.pallas import tpu as pltpu

F32 = jnp.float32
BF16 = jnp.bfloat16

EPS = 1e-6
CONV_K = 3
SSM_GROUP = 16
SSM_STATE = 64
MEM_HEADS = 4
MEM_HEAD_DIM = 128

MXU_TILE = 256
SUBLANES = 8
LANES = 128
VMEM_LIMIT_BYTES = 56 * 1024 * 1024

ROW_TILE = 512
TIME_TILE = 256
SSM_CHUNK = 4
QUAD_CH = MXU_TILE // SSM_CHUNK
SCAN_COLS = 512
SCAN_UNROLL = 8
SHUFFLE_UNROLL = 8
POST_ROWS = 512


def _rms(x, g):
    ms = jnp.mean(x * x, axis=-1, keepdims=True)
    return x * lax.rsqrt(ms + EPS) * g


def _dot(a, b):
    return jnp.dot(a, b, preferred_element_type=F32)


def _const_spec(shape):
    nd = len(shape)
    return pl.BlockSpec(shape, lambda *_: (0,) * nd, pipeline_mode=pl.Buffered(1))


def _ssm_prep_kernel(are_ref, aim_ref, ldt_ref, bre_ref, bim_ref, cre_ref, cim_ref,
                     lcre_ref, lcim_ref, ere_ref, eim_ref, fre_ref, fim_ref, kk_ref):
    a_re = are_ref[...]
    a_im = aim_ref[...]
    dt = jnp.exp(ldt_ref[...])
    mag = jnp.exp(a_re * dt)
    l_re = mag * jnp.cos(a_im * dt)
    l_im = mag * jnp.sin(a_im * dt)
    n_re = l_re - 1.0
    n_im = l_im
    den = a_re * a_re + a_im * a_im
    q_re = ((n_re * a_re + n_im * a_im) / den)[:, None, :]
    q_im = ((n_im * a_re - n_re * a_im) / den)[:, None, :]
    b_re = bre_ref[...]
    b_im = bim_ref[...]
    bb_re = q_re * b_re - q_im * b_im
    bb_im = q_re * b_im + q_im * b_re
    c_re = cre_ref[...]
    c_im = cim_ref[...]
    g, h, p = c_re.shape

    pows = [(jnp.ones_like(l_re), jnp.zeros_like(l_re))]
    for _ in range(SSM_CHUNK):
        pr, pi = pows[-1]
        pows.append((pr * l_re - pi * l_im, pr * l_im + pi * l_re))
    lcre_ref[...] = pows[SSM_CHUNK][0]
    lcim_ref[...] = pows[SSM_CHUNK][1]

    for k in range(SSM_CHUNK):
        pr, pi = pows[SSM_CHUNK - 1 - k]
        pr, pi = pr[:, None, :], pi[:, None, :]
        ere_ref[k] = pr * bb_re - pi * bb_im
        eim_ref[k] = pr * bb_im + pi * bb_re
    for r in range(SSM_CHUNK):
        pr, pi = pows[r + 1]
        pr, pi = pr[:, None, :], pi[:, None, :]
        fre_ref[r] = c_re * pr - c_im * pi
        fim_ref[r] = -(c_re * pi + c_im * pr)
    nt = (((1,), (1,)), ((), ()))
    c_re2 = c_re.reshape(g * h, p)
    c_im2 = c_im.reshape(g * h, p)
    for j in range(SSM_CHUNK):
        pr, pi = pows[j]
        pr, pi = pr[:, None, :], pi[:, None, :]
        w_re = (pr * bb_re - pi * bb_im).reshape(g * h, p)
        w_im = (pr * bb_im + pi * bb_re).reshape(g * h, p)
        kk_ref[j] = (lax.dot_general(c_re2, w_re, nt, precision=lax.Precision.HIGHEST,
                                     preferred_element_type=F32)
                     - lax.dot_general(c_im2, w_im, nt, precision=lax.Precision.HIGHEST,
                                       preferred_element_type=F32))


def _ssm_prep(a_re, a_im, log_dt, b_re_t, b_im_t, c_re, c_im):
    g, p = a_re.shape
    h = b_re_t.shape[1]
    n = SSM_CHUNK
    return pl.pallas_call(
        _ssm_prep_kernel,
        out_shape=(jax.ShapeDtypeStruct((g, p), F32), jax.ShapeDtypeStruct((g, p), F32),
                   jax.ShapeDtypeStruct((n, g, h, p), F32), jax.ShapeDtypeStruct((n, g, h, p), F32),
                   jax.ShapeDtypeStruct((n, g, h, p), F32), jax.ShapeDtypeStruct((n, g, h, p), F32),
                   jax.ShapeDtypeStruct((n, g * h, g * h), F32)),
        name="ssm_prep",
    )(a_re, a_im, log_dt.reshape(g, 1), b_re_t, b_im_t, c_re, c_im)


def _mem_kv_kernel(mem_ref, g_ref, wkt_ref, wv_ref, kt_ref, v_ref):
    mn = _rms(mem_ref[...], g_ref[...]).astype(BF16)
    kt = lax.dot_general(wkt_ref[...], mn, (((1,), (1,)), ((), ())),
                         preferred_element_type=F32)
    kt_ref[...] = kt.astype(BF16)
    v_ref[...] = _dot(mn, wv_ref[...]).astype(BF16)


def _mem_kv(mem, g, wkt, wv):
    b, m, d = mem.shape
    w = wv.shape[1]
    return pl.pallas_call(
        _mem_kv_kernel,
        grid=(b,),
        in_specs=[pl.BlockSpec((None, m, d), lambda i: (i, 0, 0)),
                  pl.BlockSpec((1, d), lambda i: (0, 0)),
                  pl.BlockSpec((w, d), lambda i: (0, 0)),
                  pl.BlockSpec((d, w), lambda i: (0, 0))],
        out_specs=(pl.BlockSpec((None, w, m), lambda i: (i, 0, 0)),
                   pl.BlockSpec((None, m, w), lambda i: (i, 0, 0))),
        out_shape=(jax.ShapeDtypeStruct((b, w, m), BF16), jax.ShapeDtypeStruct((b, m, w), BF16)),
        name="mem_kv",
    )(mem, g, wkt, wv)


def _mixer_kernel(x_ref, g1_ref, win_ref, convw_ref, convp_ref, kt_ref, v_ref, attnp_ref,
                  u_ref, sgs_ref, part_ref, cv_ref, *, cw, sw, mw, d):
    i = pl.program_id(1)
    tb = x_ref.shape[0]
    h = _rms(x_ref[...], g1_ref[...]).astype(BF16)

    def proj(lo, hi):
        return _dot(h, win_ref[:, lo:hi])

    o_u = 3 * cw
    o_q = o_u + sw
    o_gc = o_q + mw
    o_gs = o_gc + d
    o_gm = o_gs + d

    cv = proj(cw, 2 * cw) * proj(2 * cw, 3 * cw)

    @pl.when(i == 0)
    def _():
        cv_ref[0:SUBLANES, :] = jnp.zeros((SUBLANES, cw), F32)

    cv_ref[SUBLANES:SUBLANES + tb, :] = cv
    conv = (convw_ref[0:1, :] * cv_ref[SUBLANES - 2:SUBLANES - 2 + tb, :]
            + convw_ref[1:2, :] * cv_ref[SUBLANES - 1:SUBLANES - 1 + tb, :]
            + convw_ref[2:3, :] * cv)
    cv_ref[0:SUBLANES, :] = cv_ref[tb:tb + SUBLANES, :]
    pre = (proj(0, cw) * conv).astype(BF16)
    part = jax.nn.sigmoid(proj(o_gc, o_gs)) * _dot(pre, convp_ref[...])

    q = proj(o_q, o_gc)
    scale = MEM_HEAD_DIM ** -0.5
    heads = []
    for hd in range(MEM_HEADS):
        lo = hd * MEM_HEAD_DIM
        hi = lo + MEM_HEAD_DIM
        s = _dot(q[:, lo:hi].astype(BF16), kt_ref[lo:hi, :]) * scale
        e = jnp.exp(s - jnp.max(s, axis=-1, keepdims=True))
        p = e / jnp.sum(e, axis=-1, keepdims=True)
        heads.append(_dot(p.astype(BF16), v_ref[:, lo:hi]))
    o = jnp.concatenate(heads, axis=-1).astype(BF16)
    part = part + jax.nn.sigmoid(proj(o_gm, o_gm + d)) * _dot(o, attnp_ref[...])

    part_ref[...] = part.astype(BF16)
    sgs_ref[...] = jax.nn.sigmoid(proj(o_gs, o_gm)).astype(BF16)
    u_ref[...] = proj(o_u, o_q).astype(BF16)


def _mixer(x, g1, w_in, conv_w, conv_p, kt, v, attn_p, *, cw, sw, mw):
    b, s, d = x.shape
    tb = ROW_TILE
    m = kt.shape[2]
    kern = functools.partial(_mixer_kernel, cw=cw, sw=sw, mw=mw, d=d)
    return pl.pallas_call(
        kern,
        grid=(b, s // tb),
        in_specs=[pl.BlockSpec((None, tb, d), lambda bi, i: (bi, i, 0)),
                  _const_spec((1, d)),
                  _const_spec(w_in.shape),
                  _const_spec(conv_w.shape),
                  _const_spec(conv_p.shape),
                  pl.BlockSpec((None, mw, m), lambda bi, i: (bi, 0, 0)),
                  pl.BlockSpec((None, m, mw), lambda bi, i: (bi, 0, 0)),
                  _const_spec(attn_p.shape)],
        out_specs=(pl.BlockSpec((None, tb, sw), lambda bi, i: (bi, i, 0)),
                   pl.BlockSpec((None, tb, d), lambda bi, i: (bi, i, 0)),
                   pl.BlockSpec((None, tb, d), lambda bi, i: (bi, i, 0))),
        out_shape=(jax.ShapeDtypeStruct((b, s, sw), BF16),
                   jax.ShapeDtypeStruct((b, s, d), BF16),
                   jax.ShapeDtypeStruct((b, s, d), BF16)),
        scratch_shapes=[pltpu.VMEM((tb + SUBLANES, cw), F32)],
        compiler_params=pltpu.CompilerParams(
            dimension_semantics=("arbitrary", "arbitrary"),
            vmem_limit_bytes=VMEM_LIMIT_BYTES),
        name="mixer",
    )(x, g1, w_in, conv_w, conv_p, kt, v, attn_p)


def _s5_kernel(u_ref, ere_ref, eim_ref, lcre_ref, lcim_ref, fre_ref, fim_ref, tz_ref,
               gw_ref, gb_ref, sp_ref, y_ref,
               utm_ref, x_ref, st_ref, hst_ref, yc_ref, ytm_ref, *, n_state):
    batch, n_t, sw = u_ref.shape
    n_c = n_t // SSM_CHUNK
    n_q = sw // QUAD_CH
    tile = SSM_CHUNK * QUAD_CH
    half = LANES // 2
    lo = lax.broadcasted_iota(jnp.int32, (batch, LANES), 1) < half

    @pl.when(pl.program_id(0) == 0)
    def _():
        hst_ref[...] = jnp.zeros_like(hst_ref)

    utm_ref[...] = jnp.swapaxes(u_ref[...].astype(F32), 0, 1).reshape(n_t * batch, sw)

    def gather_chunk(c, carry):
        r0 = pl.multiple_of(c * batch, batch)
        for lc in range(sw // LANES):
            for kp in range(SSM_CHUNK // 2):
                t0 = pl.multiple_of((c * SSM_CHUNK + 2 * kp) * batch, batch)
                s0 = utm_ref[pl.ds(t0, batch), pl.ds(lc * LANES, LANES)]
                s1 = utm_ref[pl.ds(t0 + batch, batch), pl.ds(lc * LANES, LANES)]
                col = kp * LANES
                x_ref[pl.ds(r0, batch), pl.ds((2 * lc) * tile + col, LANES)] = jnp.where(
                    lo, s0, pltpu.roll(s1, half, 1))
                x_ref[pl.ds(r0, batch), pl.ds((2 * lc + 1) * tile + col, LANES)] = jnp.where(
                    lo, pltpu.roll(s0, half, 1), s1)
        return carry

    lax.fori_loop(0, n_c, gather_chunk, 0, unroll=SHUFFLE_UNROLL)

    for q in range(n_q):
        xq = x_ref[:, q * tile:(q + 1) * tile].astype(BF16)
        st_ref[:, q * tile:(q + 1) * tile] = _dot(xq, ere_ref[q])
        st_ref[:, n_state + q * tile:n_state + (q + 1) * tile] = _dot(xq, eim_ref[q])

    for cc in range(n_state // SCAN_COLS):
        re = slice(cc * SCAN_COLS, (cc + 1) * SCAN_COLS)
        im = slice(n_state + cc * SCAN_COLS, n_state + (cc + 1) * SCAN_COLS)
        l_re = lcre_ref[:, re]
        l_im = lcim_ref[:, re]

        def step(c, carry):
            h_re, h_im = carry
            r0 = pl.multiple_of(c * batch, batch)
            b_re = st_ref[pl.ds(r0, batch), re]
            b_im = st_ref[pl.ds(r0, batch), im]
            st_ref[pl.ds(r0, batch), re] = h_re
            st_ref[pl.ds(r0, batch), im] = h_im
            return (l_re * h_re - l_im * h_im + b_re, l_re * h_im + l_im * h_re + b_im)

        h_re, h_im = lax.fori_loop(0, n_c, step, (hst_ref[:, re], hst_ref[:, im]),
                                   unroll=SCAN_UNROLL)
        hst_ref[:, re] = h_re
        hst_ref[:, im] = h_im

    for q in range(n_q):
        cols = slice(q * tile, (q + 1) * tile)
        xq = x_ref[:, cols].astype(BF16)
        h_re = st_ref[:, cols].astype(BF16)
        h_im = st_ref[:, n_state + q * tile:n_state + (q + 1) * tile].astype(BF16)
        yc_ref[:, cols] = (_dot(h_re, fre_ref[q]) + _dot(h_im, fim_ref[q]) + _dot(xq, tz_ref[q]))

    def scatter_chunk(c, carry):
        r0 = pl.multiple_of(c * batch, batch)
        for j in range(sw // LANES):
            for rp in range(SSM_CHUNK // 2):
                col = rp * LANES
                a = yc_ref[pl.ds(r0, batch), pl.ds((2 * j) * tile + col, LANES)]
                b = yc_ref[pl.ds(r0, batch), pl.ds((2 * j + 1) * tile + col, LANES)]
                t0 = pl.multiple_of((c * SSM_CHUNK + 2 * rp) * batch, batch)
                ytm_ref[pl.ds(t0, batch), pl.ds(j * LANES, LANES)] = jnp.where(
                    lo, a, pltpu.roll(b, half, 1))
                ytm_ref[pl.ds(t0 + batch, batch), pl.ds(j * LANES, LANES)] = jnp.where(
                    lo, pltpu.roll(a, half, 1), b)
        return carry

    lax.fori_loop(0, n_c, scatter_chunk, 0, unroll=SHUFFLE_UNROLL)

    sub_t = POST_ROWS // batch
    for sb in range(n_t // sub_t):
        y = ytm_ref[sb * POST_ROWS:(sb + 1) * POST_ROWS, :]
        y = jax.nn.gelu(y)
        y = y * jax.nn.sigmoid(_dot(y.astype(BF16), gw_ref[...]) + gb_ref[...])
        out = _dot(y.astype(BF16), sp_ref[...]).reshape(sub_t, batch, -1)
        y_ref[:, sb * sub_t:(sb + 1) * sub_t, :] = jnp.swapaxes(out, 0, 1).astype(BF16)


def _s5(u, ere, eim, lcre, lcim, fre, fim, tz, glu_w, glu_b, ssm_p):
    batch, s, sw = u.shape
    d = ssm_p.shape[1]
    n_state = lcre.shape[1]
    n_t = TIME_TILE
    rows_c = (n_t // SSM_CHUNK) * batch
    kern = functools.partial(_s5_kernel, n_state=n_state)
    consts = (ere, eim, lcre, lcim, fre, fim, tz, glu_w, glu_b, ssm_p)
    return pl.pallas_call(
        kern,
        grid=(s // n_t,),
        in_specs=[pl.BlockSpec((batch, n_t, sw), lambda i: (0, i, 0))]
                 + [_const_spec(c.shape) for c in consts],
        out_specs=pl.BlockSpec((batch, n_t, d), lambda i: (0, i, 0)),
        out_shape=jax.ShapeDtypeStruct((batch, s, d), BF16),
        scratch_shapes=[pltpu.VMEM((batch * n_t, sw), F32),
                        pltpu.VMEM((rows_c, SSM_CHUNK * sw), F32),
                        pltpu.VMEM((rows_c, 2 * n_state), F32),
                        pltpu.VMEM((batch, 2 * n_state), F32),
                        pltpu.VMEM((rows_c, SSM_CHUNK * sw), F32),
                        pltpu.VMEM((batch * n_t, sw), F32)],
        compiler_params=pltpu.CompilerParams(
            dimension_semantics=("arbitrary",),
            vmem_limit_bytes=VMEM_LIMIT_BYTES),
        name="s5",
    )(u, *consts)


def _ffn_kernel(x_ref, part_ref, sgs_ref, yssm_ref, wo_ref, g2_ref, wg_ref, wu_ref, wd_ref,
                gf_ref, o_ref, *, chunks):
    merged = part_ref[...].astype(F32) + sgs_ref[...].astype(F32) * yssm_ref[...].astype(F32)
    x1 = x_ref[...] + _dot(merged.astype(BF16), wo_ref[...])
    h2 = _rms(x1, g2_ref[...]).astype(BF16)
    acc = x1
    for lo, hi in chunks:
        g = _dot(h2, wg_ref[:, lo:hi])
        a = (g * jax.nn.sigmoid(g) * _dot(h2, wu_ref[:, lo:hi])).astype(BF16)
        acc = acc + _dot(a, wd_ref[lo:hi, :])
    o_ref[...] = _rms(acc, gf_ref[...])


def _ffn_chunks(hidden):
    step = 4 * MXU_TILE
    return tuple((lo, min(lo + step, hidden)) for lo in range(0, hidden, step))


def _ffn(x, part, sgs, yssm, w_o, g2, wg, wu, wd, gf):
    b, s, d = x.shape
    tb = ROW_TILE
    kern = functools.partial(_ffn_kernel, chunks=_ffn_chunks(wg.shape[1]))
    row_spec = pl.BlockSpec((None, tb, d), lambda bi, i: (bi, i, 0))
    return pl.pallas_call(
        kern,
        grid=(b, s // tb),
        in_specs=[row_spec, row_spec, row_spec, row_spec,
                  _const_spec(w_o.shape), _const_spec((1, d)),
                  _const_spec(wg.shape), _const_spec(wu.shape), _const_spec(wd.shape),
                  _const_spec((1, d))],
        out_specs=row_spec,
        out_shape=jax.ShapeDtypeStruct((b, s, d), x.dtype),
        compiler_params=pltpu.CompilerParams(
            dimension_semantics=("arbitrary", "arbitrary"),
            vmem_limit_bytes=VMEM_LIMIT_BYTES),
        name="ffn",
    )(x, part, sgs, yssm, w_o, g2, wg, wu, wd, gf)


def _quad_eye(n):
    return jnp.eye(n, dtype=F32)


def _pack_e(e):
    k, g, h, p = e.shape
    a = QUAD_CH // h
    e5 = e.reshape(k, g // a, a, h, p)
    t = e5[:, :, :, :, None, :] * _quad_eye(a)[None, None, :, None, :, None]
    return t.transpose(1, 0, 2, 3, 4, 5).reshape(g // a, k * a * h, a * p)


def _pack_f(f):
    r, g, h, p = f.shape
    a = QUAD_CH // h
    f5 = f.reshape(r, g // a, a, h, p)
    t = f5[:, :, :, :, :, None] * _quad_eye(a)[None, None, :, None, None, :]
    return t.transpose(1, 2, 4, 0, 5, 3).reshape(g // a, a * p, r * a * h)


def _pack_toeplitz(kk, d_skip, g, h):
    n = kk.shape[0]
    a = QUAD_CH // h
    k5 = kk.reshape(n, g, h, g, h)
    kd = jnp.stack([k5[:, gi, :, gi, :] for gi in range(g)], axis=1)
    kd = kd.at[0].add(d_skip.reshape(g, h)[:, :, None] * jnp.eye(h, dtype=F32))
    zero = jnp.zeros_like(kd[0])
    ksel = jnp.stack([jnp.stack([kd[r - k] if r >= k else zero for r in range(n)])
                      for k in range(n)])
    k7 = ksel.reshape(n, n, g // a, a, h, h)
    t = k7[..., None] * _quad_eye(a)[None, None, None, :, None, None, :]
    return t.transpose(2, 0, 3, 5, 1, 6, 4).reshape(g // a, n * a * h, n * a * h)


def kernel(x, mem, norm1_g, w_in, conv_w, conv_proj, ssm_A_re, ssm_A_im, ssm_log_dt, ssm_B_re,
           ssm_B_im, ssm_C_re, ssm_C_im, ssm_D, ssm_glu_w, ssm_glu_b, ssm_proj, mem_norm_g,
           attn_wk, attn_wv, attn_proj, w_o, norm2_g, ffn_w_gate, ffn_w_up, ffn_w_down,
           final_norm_g):
    b, s, d = x.shape
    depth = w_in.shape[0]
    cw = conv_w.shape[2]
    sw = ssm_D.shape[1]
    mw = attn_wk.shape[2]
    n_groups = ssm_A_re.shape[1]
    n_state = n_groups * SSM_STATE
    assert depth == 1, "the final norm is fused into the single layer's ffn kernel"
    assert b == SUBLANES, "the S5 recurrence keeps the batch on the sublane axis"

    for l in range(depth):
        lc_re, lc_im, e_re, e_im, f_re, f_im, kk = _ssm_prep(
            ssm_A_re[l], ssm_A_im[l], ssm_log_dt[l],
            ssm_B_re[l].transpose(0, 2, 1), ssm_B_im[l].transpose(0, 2, 1),
            ssm_C_re[l], ssm_C_im[l])
        ere = _pack_e(e_re).astype(BF16)
        eim = _pack_e(e_im).astype(BF16)
        fre = _pack_f(f_re).astype(BF16)
        fim = _pack_f(f_im).astype(BF16)
        tz = _pack_toeplitz(kk, ssm_D[l], n_groups, SSM_GROUP).astype(BF16)
        lcre = jnp.broadcast_to(lc_re.reshape(1, n_state), (b, n_state))
        lcim = jnp.broadcast_to(lc_im.reshape(1, n_state), (b, n_state))

        kt, v = _mem_kv(mem, mem_norm_g[l].reshape(1, d), attn_wk[l].T.astype(BF16),
                        attn_wv[l].astype(BF16))
        u, sgs, part = _mixer(
            x, norm1_g[l].reshape(1, d), w_in[l].astype(BF16), conv_w[l],
            conv_proj[l].astype(BF16), kt, v, attn_proj[l].astype(BF16), cw=cw, sw=sw, mw=mw)
        yssm = _s5(u, ere, eim, lcre, lcim, fre, fim, tz,
                   ssm_glu_w[l].astype(BF16), ssm_glu_b[l].reshape(1, sw),
                   ssm_proj[l].astype(BF16))
        x = _ffn(x, part, sgs, yssm, w_o[l].astype(BF16),
                 norm2_g[l].reshape(1, d), ffn_w_gate[l].astype(BF16), ffn_w_up[l].astype(BF16),
                 ffn_w_down[l].astype(BF16), final_norm_g.reshape(1, d))
    return x
```

```python
import functools
import math

import jax
import jax.numpy as jnp
from jax import lax
from jax.experimental import pallas as pl
from jax.experimental.pallas import tpu as pltpu

F32 = jnp.float32
BF16 = jnp.bfloat16

EPS = 1e-6
CONV_K = 3
SSM_GROUP = 16
SSM_STATE = 64
MEM_HEADS = 4
MEM_HEAD_DIM = 128

MXU_TILE = 256
SUBLANES = 8
LANES = 128
VMEM_LIMIT_BYTES = 56 * 1024 * 1024

ROW_TILE = 512
ROW_SUBTILES = 2
TIME_TILE = 256
SSM_CHUNK = 4
QUAD_CH = MXU_TILE // SSM_CHUNK
SCAN_COLS = 512
SCAN_UNROLL = 8
SHUFFLE_UNROLL = 8
POST_ROWS = 512


def _rms(x, g):
    ms = jnp.mean(x * x, axis=-1, keepdims=True)
    return x * lax.rsqrt(ms + EPS) * g


def _dot(a, b):
    return jnp.dot(a, b, preferred_element_type=F32)


def _const_spec(shape):
    nd = len(shape)
    return pl.BlockSpec(shape, lambda *_: (0,) * nd, pipeline_mode=pl.Buffered(1))


def _ssm_prep_kernel(are_ref, aim_ref, ldt_ref, bre_ref, bim_ref, cre_ref, cim_ref,
                     lcre_ref, lcim_ref, ere_ref, eim_ref, fre_ref, fim_ref, kq_ref):
    a_re = are_ref[...]
    a_im = aim_ref[...]
    dt = jnp.exp(ldt_ref[...])
    mag = jnp.exp(a_re * dt)
    l_re = mag * jnp.cos(a_im * dt)
    l_im = mag * jnp.sin(a_im * dt)
    n_re = l_re - 1.0
    n_im = l_im
    den = a_re * a_re + a_im * a_im
    q_re = ((n_re * a_re + n_im * a_im) / den)[:, None, :]
    q_im = ((n_im * a_re - n_re * a_im) / den)[:, None, :]
    b_re = bre_ref[...]
    b_im = bim_ref[...]
    bb_re = q_re * b_re - q_im * b_im
    bb_im = q_re * b_im + q_im * b_re
    c_re = cre_ref[...]
    c_im = cim_ref[...]
    g, h, p = c_re.shape

    pows = [(jnp.ones_like(l_re), jnp.zeros_like(l_re))]
    for _ in range(SSM_CHUNK):
        pr, pi = pows[-1]
        pows.append((pr * l_re - pi * l_im, pr * l_im + pi * l_re))
    lcre_ref[...] = pows[SSM_CHUNK][0]
    lcim_ref[...] = pows[SSM_CHUNK][1]

    for k in range(SSM_CHUNK):
        pr, pi = pows[SSM_CHUNK - 1 - k]
        pr, pi = pr[:, None, :], pi[:, None, :]
        ere_ref[k] = pr * bb_re - pi * bb_im
        eim_ref[k] = pr * bb_im + pi * bb_re
    for r in range(SSM_CHUNK):
        pr, pi = pows[r + 1]
        pr, pi = pr[:, None, :], pi[:, None, :]
        fre_ref[r] = c_re * pr - c_im * pi
        fim_ref[r] = -(c_re * pi + c_im * pr)
    nt = (((1,), (1,)), ((), ()))
    c_re2 = c_re.reshape(g * h, p)
    c_im2 = c_im.reshape(g * h, p)
    for j in range(SSM_CHUNK):
        pr, pi = pows[j]
        pr, pi = pr[:, None, :], pi[:, None, :]
        w_re = (pr * bb_re - pi * bb_im).reshape(g * h, p)
        w_im = (pr * bb_im + pi * bb_re).reshape(g * h, p)
        kk = (lax.dot_general(w_re, c_re2, nt, precision=lax.Precision.HIGHEST,
                              preferred_element_type=F32)
              - lax.dot_general(w_im, c_im2, nt, precision=lax.Precision.HIGHEST,
                                preferred_element_type=F32))
        for q in range(g * h // QUAD_CH):
            kq_ref[j, q] = kk[q * QUAD_CH:(q + 1) * QUAD_CH, q * QUAD_CH:(q + 1) * QUAD_CH]


def _ssm_prep(a_re, a_im, log_dt, b_re_t, b_im_t, c_re, c_im):
    g, p = a_re.shape
    h = b_re_t.shape[1]
    n = SSM_CHUNK
    return pl.pallas_call(
        _ssm_prep_kernel,
        out_shape=(jax.ShapeDtypeStruct((g, p), F32), jax.ShapeDtypeStruct((g, p), F32),
                   jax.ShapeDtypeStruct((n, g, h, p), F32), jax.ShapeDtypeStruct((n, g, h, p), F32),
                   jax.ShapeDtypeStruct((n, g, h, p), F32), jax.ShapeDtypeStruct((n, g, h, p), F32),
                   jax.ShapeDtypeStruct((n, g * h // QUAD_CH, QUAD_CH, QUAD_CH), F32)),
        name="ssm_prep",
    )(a_re, a_im, log_dt.reshape(g, 1), b_re_t, b_im_t, c_re, c_im)


def _mem_kv_kernel(mem_ref, g_ref, wkt_ref, wv_ref, kt_ref, v_ref):
    mn = _rms(mem_ref[...], g_ref[...]).astype(BF16)
    kt = lax.dot_general(wkt_ref[...], mn, (((1,), (1,)), ((), ())),
                         preferred_element_type=F32)
    kt_ref[...] = kt.astype(BF16)
    v_ref[...] = _dot(mn, wv_ref[...]).astype(BF16)


def _mem_kv(mem, g, wkt, wv):
    b, m, d = mem.shape
    w = wv.shape[1]
    return pl.pallas_call(
        _mem_kv_kernel,
        grid=(b,),
        in_specs=[pl.BlockSpec((None, m, d), lambda i: (i, 0, 0)),
                  pl.BlockSpec((1, d), lambda i: (0, 0)),
                  pl.BlockSpec((w, d), lambda i: (0, 0)),
                  pl.BlockSpec((d, w), lambda i: (0, 0))],
        out_specs=(pl.BlockSpec((None, w, m), lambda i: (i, 0, 0)),
                   pl.BlockSpec((None, m, w), lambda i: (i, 0, 0))),
        out_shape=(jax.ShapeDtypeStruct((b, w, m), BF16), jax.ShapeDtypeStruct((b, m, w), BF16)),
        name="mem_kv",
    )(mem, g, wkt, wv)


def _mixer_kernel(x_ref, g1_ref, win_ref, convw_ref, convp_ref, kt_ref, v_ref, attnp_ref,
                  u_ref, sgs_ref, part_ref, cv_ref, *, cw, sw, mw, d):
    i = pl.program_id(1)
    tb = x_ref.shape[0]
    h = _rms(x_ref[...], g1_ref[...]).astype(BF16)

    def proj(lo, hi):
        return _dot(h, win_ref[:, lo:hi])

    o_u = 3 * cw
    o_q = o_u + sw
    o_gc = o_q + mw
    o_gs = o_gc + d
    o_gm = o_gs + d

    q = proj(o_q, o_gc)
    scale = MEM_HEAD_DIM ** -0.5
    scores = []
    for hd in range(MEM_HEADS):
        lo = hd * MEM_HEAD_DIM
        scores.append(_dot(q[:, lo:lo + MEM_HEAD_DIM].astype(BF16),
                           kt_ref[lo:lo + MEM_HEAD_DIM, :]) * scale)
    c_gate = proj(cw, 2 * cw)
    v_conv = proj(2 * cw, 3 * cw)

    heads = []
    for hd in range(MEM_HEADS):
        lo = hd * MEM_HEAD_DIM
        s = scores[hd]
        e = jnp.exp(s - jnp.max(s, axis=-1, keepdims=True))
        p = e / jnp.sum(e, axis=-1, keepdims=True)
        heads.append(_dot(p.astype(BF16), v_ref[:, lo:lo + MEM_HEAD_DIM]))
    o = jnp.concatenate(heads, axis=-1).astype(BF16)
    b_gate = proj(0, cw)

    cv = c_gate * v_conv

    @pl.when(i == 0)
    def _():
        cv_ref[0:SUBLANES, :] = jnp.zeros((SUBLANES, cw), F32)

    cv_ref[SUBLANES:SUBLANES + tb, :] = cv
    conv = (convw_ref[0:1, :] * cv_ref[SUBLANES - 2:SUBLANES - 2 + tb, :]
            + convw_ref[1:2, :] * cv_ref[SUBLANES - 1:SUBLANES - 1 + tb, :]
            + convw_ref[2:3, :] * cv)
    cv_ref[0:SUBLANES, :] = cv_ref[tb:tb + SUBLANES, :]
    pre = (b_gate * conv).astype(BF16)

    g_c = proj(o_gc, o_gs)
    y_mem = _dot(o, attnp_ref[...])
    y_conv = _dot(pre, convp_ref[...])
    g_m = proj(o_gm, o_gm + d)
    part_ref[...] = (jax.nn.sigmoid(g_c) * y_conv + jax.nn.sigmoid(g_m) * y_mem).astype(BF16)
    sgs_ref[...] = jax.nn.sigmoid(proj(o_gs, o_gm)).astype(BF16)
    u_ref[...] = proj(o_u, o_q).astype(BF16)


def _mixer(x, g1, w_in, conv_w, conv_p, kt, v, attn_p, *, cw, sw, mw):
    b, s, d = x.shape
    tb = ROW_TILE
    m = kt.shape[2]
    kern = functools.partial(_mixer_kernel, cw=cw, sw=sw, mw=mw, d=d)
    return pl.pallas_call(
        kern,
        grid=(b, s // tb),
        in_specs=[pl.BlockSpec((None, tb, d), lambda bi, i: (bi, i, 0)),
                  _const_spec((1, d)),
                  _const_spec(w_in.shape),
                  _const_spec(conv_w.shape),
                  _const_spec(conv_p.shape),
                  pl.BlockSpec((None, mw, m), lambda bi, i: (bi, 0, 0)),
                  pl.BlockSpec((None, m, mw), lambda bi, i: (bi, 0, 0)),
                  _const_spec(attn_p.shape)],
        out_specs=(pl.BlockSpec((None, tb, sw), lambda bi, i: (bi, i, 0)),
                   pl.BlockSpec((None, tb, d), lambda bi, i: (bi, i, 0)),
                   pl.BlockSpec((None, tb, d), lambda bi, i: (bi, i, 0))),
        out_shape=(jax.ShapeDtypeStruct((b, s, sw), BF16),
                   jax.ShapeDtypeStruct((b, s, d), BF16),
                   jax.ShapeDtypeStruct((b, s, d), BF16)),
        scratch_shapes=[pltpu.VMEM((tb + SUBLANES, cw), F32)],
        compiler_params=pltpu.CompilerParams(
            dimension_semantics=("arbitrary", "arbitrary"),
            vmem_limit_bytes=VMEM_LIMIT_BYTES),
        name="mixer",
    )(x, g1, w_in, conv_w, conv_p, kt, v, attn_p)


def _s5_kernel(u_ref, ere_ref, eim_ref, lcre_ref, lcim_ref, fre_ref, fim_ref, tz_ref,
               gw_ref, gb_ref, sp_ref, y_ref,
               utm_ref, x_ref, st_ref, hst_ref, yc_ref, ytm_ref, *, n_state):
    batch, n_t, sw = u_ref.shape
    n_c = n_t // SSM_CHUNK
    n_q = sw // QUAD_CH
    tile = SSM_CHUNK * QUAD_CH
    half = LANES // 2
    lo = lax.broadcasted_iota(jnp.int32, (batch, LANES), 1) < half

    @pl.when(pl.program_id(0) == 0)
    def _():
        hst_ref[...] = jnp.zeros_like(hst_ref)

    utm_ref[...] = jnp.swapaxes(u_ref[...].astype(F32), 0, 1).reshape(n_t * batch, sw)

    def gather_chunk(c, carry):
        r0 = pl.multiple_of(c * batch, batch)
        for lc in range(sw // LANES):
            for kp in range(SSM_CHUNK // 2):
                t0 = pl.multiple_of((c * SSM_CHUNK + 2 * kp) * batch, batch)
                s0 = utm_ref[pl.ds(t0, batch), pl.ds(lc * LANES, LANES)]
                s1 = utm_ref[pl.ds(t0 + batch, batch), pl.ds(lc * LANES, LANES)]
                col = kp * LANES
                x_ref[pl.ds(r0, batch), pl.ds((2 * lc) * tile + col, LANES)] = jnp.where(
                    lo, s0, pltpu.roll(s1, half, 1))
                x_ref[pl.ds(r0, batch), pl.ds((2 * lc + 1) * tile + col, LANES)] = jnp.where(
                    lo, pltpu.roll(s0, half, 1), s1)
        return carry

    lax.fori_loop(0, n_c, gather_chunk, 0, unroll=SHUFFLE_UNROLL)

    for q in range(n_q):
        xq = x_ref[:, q * tile:(q + 1) * tile].astype(BF16)
        st_ref[:, q * tile:(q + 1) * tile] = _dot(xq, ere_ref[q])
        st_ref[:, n_state + q * tile:n_state + (q + 1) * tile] = _dot(xq, eim_ref[q])

    for cc in range(n_state // SCAN_COLS):
        re = slice(cc * SCAN_COLS, (cc + 1) * SCAN_COLS)
        im = slice(n_state + cc * SCAN_COLS, n_state + (cc + 1) * SCAN_COLS)
        l_re = lcre_ref[:, re]
        l_im = lcim_ref[:, re]

        def step(c, carry):
            h_re, h_im = carry
            r0 = pl.multiple_of(c * batch, batch)
            b_re = st_ref[pl.ds(r0, batch), re]
            b_im = st_ref[pl.ds(r0, batch), im]
            st_ref[pl.ds(r0, batch), re] = h_re
            st_ref[pl.ds(r0, batch), im] = h_im
            return (l_re * h_re - l_im * h_im + b_re, l_re * h_im + l_im * h_re + b_im)

        h_re, h_im = lax.fori_loop(0, n_c, step, (hst_ref[:, re], hst_ref[:, im]),
                                   unroll=SCAN_UNROLL)
        hst_ref[:, re] = h_re
        hst_ref[:, im] = h_im

    for q in range(n_q):
        cols = slice(q * tile, (q + 1) * tile)
        xq = x_ref[:, cols].astype(BF16)
        h_re = st_ref[:, cols].astype(BF16)
        h_im = st_ref[:, n_state + q * tile:n_state + (q + 1) * tile].astype(BF16)
        yc_ref[:, cols] = (_dot(h_re, fre_ref[q]) + _dot(h_im, fim_ref[q]) + _dot(xq, tz_ref[q]))

    def scatter_chunk(c, carry):
        r0 = pl.multiple_of(c * batch, batch)
        for j in range(sw // LANES):
            for rp in range(SSM_CHUNK // 2):
                col = rp * LANES
                a = yc_ref[pl.ds(r0, batch), pl.ds((2 * j) * tile + col, LANES)]
                b = yc_ref[pl.ds(r0, batch), pl.ds((2 * j + 1) * tile + col, LANES)]
                t0 = pl.multiple_of((c * SSM_CHUNK + 2 * rp) * batch, batch)
                ytm_ref[pl.ds(t0, batch), pl.ds(j * LANES, LANES)] = jnp.where(
                    lo, a, pltpu.roll(b, half, 1))
                ytm_ref[pl.ds(t0 + batch, batch), pl.ds(j * LANES, LANES)] = jnp.where(
                    lo, pltpu.roll(a, half, 1), b)
        return carry

    lax.fori_loop(0, n_c, scatter_chunk, 0, unroll=SHUFFLE_UNROLL)

    sub_t = POST_ROWS // batch
    for sb in range(n_t // sub_t):
        y = ytm_ref[sb * POST_ROWS:(sb + 1) * POST_ROWS, :]
        y = jax.nn.gelu(y)
        y = y * jax.nn.sigmoid(_dot(y.astype(BF16), gw_ref[...]) + gb_ref[...])
        out = _dot(y.astype(BF16), sp_ref[...]).reshape(sub_t, batch, -1)
        y_ref[:, sb * sub_t:(sb + 1) * sub_t, :] = jnp.swapaxes(out, 0, 1).astype(BF16)


def _s5(u, ere, eim, lcre, lcim, fre, fim, tz, glu_w, glu_b, ssm_p):
    batch, s, sw = u.shape
    d = ssm_p.shape[1]
    n_state = lcre.shape[1]
    n_t = TIME_TILE
    rows_c = (n_t // SSM_CHUNK) * batch
    kern = functools.partial(_s5_kernel, n_state=n_state)
    consts = (ere, eim, lcre, lcim, fre, fim, tz, glu_w, glu_b, ssm_p)
    return pl.pallas_call(
        kern,
        grid=(s // n_t,),
        in_specs=[pl.BlockSpec((batch, n_t, sw), lambda i: (0, i, 0))]
                 + [_const_spec(c.shape) for c in consts],
        out_specs=pl.BlockSpec((batch, n_t, d), lambda i: (0, i, 0)),
        out_shape=jax.ShapeDtypeStruct((batch, s, d), BF16),
        scratch_shapes=[pltpu.VMEM((batch * n_t, sw), F32),
                        pltpu.VMEM((rows_c, SSM_CHUNK * sw), F32),
                        pltpu.VMEM((rows_c, 2 * n_state), F32),
                        pltpu.VMEM((batch, 2 * n_state), F32),
                        pltpu.VMEM((rows_c, SSM_CHUNK * sw), F32),
                        pltpu.VMEM((batch * n_t, sw), F32)],
        compiler_params=pltpu.CompilerParams(
            dimension_semantics=("arbitrary",),
            vmem_limit_bytes=VMEM_LIMIT_BYTES),
        name="s5",
    )(u, *consts)


def _ffn_kernel(x_ref, part_ref, sgs_ref, yssm_ref, wo_ref, g2_ref, wg_ref, wu_ref, wd_ref,
                gf_ref, o_ref, *, chunks):
    tb = x_ref.shape[0] // ROW_SUBTILES
    subs = [slice(sub * tb, (sub + 1) * tb) for sub in range(ROW_SUBTILES)]
    x1s = []
    for rows in subs:
        merged = (part_ref[rows, :].astype(F32)
                  + sgs_ref[rows, :].astype(F32) * yssm_ref[rows, :].astype(F32))
        x1s.append(x_ref[rows, :] + _dot(merged.astype(BF16), wo_ref[...]))
    h2s = [_rms(x1, g2_ref[...]).astype(BF16) for x1 in x1s]
    outs = []
    for x1, h2 in zip(x1s, h2s):
        acc = x1
        for lo, hi in chunks:
            g = _dot(h2, wg_ref[:, lo:hi])
            a = (g * jax.nn.sigmoid(g) * _dot(h2, wu_ref[:, lo:hi])).astype(BF16)
            acc = acc + _dot(a, wd_ref[lo:hi, :])
        outs.append(_rms(acc, gf_ref[...]))
    o_ref[...] = jnp.concatenate(outs, axis=0)


def _ffn_chunks(hidden):
    step = 4 * MXU_TILE
    return tuple((lo, min(lo + step, hidden)) for lo in range(0, hidden, step))


def _ffn(x, part, sgs, yssm, w_o, g2, wg, wu, wd, gf):
    b, s, d = x.shape
    tb = ROW_TILE
    kern = functools.partial(_ffn_kernel, chunks=_ffn_chunks(wg.shape[1]))
    row_spec = pl.BlockSpec((None, tb, d), lambda bi, i: (bi, i, 0))
    return pl.pallas_call(
        kern,
        grid=(b, s // tb),
        in_specs=[row_spec, row_spec, row_spec, row_spec,
                  _const_spec(w_o.shape), _const_spec((1, d)),
                  _const_spec(wg.shape), _const_spec(wu.shape), _const_spec(wd.shape),
                  _const_spec((1, d))],
        out_specs=row_spec,
        out_shape=jax.ShapeDtypeStruct((b, s, d), x.dtype),
        compiler_params=pltpu.CompilerParams(
            dimension_semantics=("arbitrary", "arbitrary"),
            vmem_limit_bytes=VMEM_LIMIT_BYTES),
        name="ffn",
    )(x, part, sgs, yssm, w_o, g2, wg, wu, wd, gf)


def _quad_eye(n):
    return jnp.eye(n, dtype=F32)


def _pack_e(e):
    k, g, h, p = e.shape
    a = QUAD_CH // h
    e5 = e.reshape(k, g // a, a, h, p)
    t = e5[:, :, :, :, None, :] * _quad_eye(a)[None, None, :, None, :, None]
    return t.transpose(1, 0, 2, 3, 4, 5).reshape(g // a, k * a * h, a * p)


def _pack_f(f):
    return _pack_e(f).transpose(0, 2, 1)


def _pack_toeplitz(kq, d_skip, h):
    n, n_q, qc, _ = kq.shape
    idx = jnp.arange(qc)
    same_group = (idx[:, None] // h == idx[None, :] // h).astype(F32)
    blk = kq * same_group
    blk = blk.at[0].add(jnp.eye(qc, dtype=F32) * d_skip.reshape(n_q, 1, qc))
    zero = jnp.zeros_like(blk[0])
    rows = [jnp.concatenate([blk[r - k] if r >= k else zero for r in range(n)], axis=-1)
            for k in range(n)]
    return jnp.concatenate(rows, axis=1)


def kernel(x, mem, norm1_g, w_in, conv_w, conv_proj, ssm_A_re, ssm_A_im, ssm_log_dt, ssm_B_re,
           ssm_B_im, ssm_C_re, ssm_C_im, ssm_D, ssm_glu_w, ssm_glu_b, ssm_proj, mem_norm_g,
           attn_wk, attn_wv, attn_proj, w_o, norm2_g, ffn_w_gate, ffn_w_up, ffn_w_down,
           final_norm_g):
    b, s, d = x.shape
    depth = w_in.shape[0]
    cw = conv_w.shape[2]
    sw = ssm_D.shape[1]
    mw = attn_wk.shape[2]
    n_groups = ssm_A_re.shape[1]
    n_state = n_groups * SSM_STATE
    assert depth == 1, "the final norm is fused into the single layer's ffn kernel"
    assert b == SUBLANES, "the S5 recurrence keeps the batch on the sublane axis"

    for l in range(depth):
        lc_re, lc_im, e_re, e_im, f_re, f_im, kq = _ssm_prep(
            ssm_A_re[l], ssm_A_im[l], ssm_log_dt[l],
            ssm_B_re[l].transpose(0, 2, 1), ssm_B_im[l].transpose(0, 2, 1),
            ssm_C_re[l], ssm_C_im[l])
        ere = _pack_e(e_re).astype(BF16)
        eim = _pack_e(e_im).astype(BF16)
        fre = _pack_f(f_re).astype(BF16)
        fim = _pack_f(f_im).astype(BF16)
        tz = _pack_toeplitz(kq, ssm_D[l], SSM_GROUP).astype(BF16)
        lcre = jnp.broadcast_to(lc_re.reshape(1, n_state), (b, n_state))
        lcim = jnp.broadcast_to(lc_im.reshape(1, n_state), (b, n_state))

        kt, v = _mem_kv(mem, mem_norm_g[l].reshape(1, d), attn_wk[l].T.astype(BF16),
                        attn_wv[l].astype(BF16))
        u, sgs, part = _mixer(
            x, norm1_g[l].reshape(1, d), w_in[l].astype(BF16), conv_w[l],
            conv_proj[l].astype(BF16), kt, v, attn_proj[l].astype(BF16), cw=cw, sw=sw, mw=mw)
        yssm = _s5(u, ere, eim, lcre, lcim, fre, fim, tz,
                   ssm_glu_w[l].astype(BF16), ssm_glu_b[l].reshape(1, sw),
                   ssm_proj[l].astype(BF16))
        x = _ffn(x, part, sgs, yssm, w_o[l].astype(BF16),
                 norm2_g[l].reshape(1, d), ffn_w_gate[l].astype(BF16), ffn_w_up[l].astype(BF16),
                 ffn_w_down[l].astype(BF16), final_norm_g.reshape(1, d))
    return x
```

```python
import functools
import math

import jax
import jax.numpy as jnp
from jax import lax
from jax.experimental import pallas as pl
from jax.experimental.pallas import tpu as pltpu

F32 = jnp.float32
BF16 = jnp.bfloat16

EPS = 1e-6
CONV_K = 3
SSM_GROUP = 16
SSM_STATE = 64
MEM_HEADS = 4
MEM_HEAD_DIM = 128

MXU_TILE = 256
SUBLANES = 8
LANES = 128
VMEM_LIMIT_BYTES = 56 * 1024 * 1024

MIXER_ROW_TILE = 1024
ROW_TILE = 512
ROW_SUBTILES = 2
TIME_TILE = 256
SSM_CHUNK = 4
QUAD_CH = MXU_TILE // SSM_CHUNK
SCAN_COLS = 512
SCAN_UNROLL = 8
SHUFFLE_UNROLL = 8
POST_ROWS = 512


def _rms(x, g):
    ms = jnp.mean(x * x, axis=-1, keepdims=True)
    return x * lax.rsqrt(ms + EPS) * g


def _dot(a, b):
    return jnp.dot(a, b, preferred_element_type=F32)


def _const_spec(shape):
    nd = len(shape)
    return pl.BlockSpec(shape, lambda *_: (0,) * nd, pipeline_mode=pl.Buffered(1))


def _ssm_prep_kernel(are_ref, aim_ref, ldt_ref, bre_ref, bim_ref, cre_ref, cim_ref,
                     lcre_ref, lcim_ref, ere_ref, eim_ref, fre_ref, fim_ref, kq_ref):
    a_re = are_ref[...]
    a_im = aim_ref[...]
    dt = jnp.exp(ldt_ref[...])
    mag = jnp.exp(a_re * dt)
    l_re = mag * jnp.cos(a_im * dt)
    l_im = mag * jnp.sin(a_im * dt)
    n_re = l_re - 1.0
    n_im = l_im
    den = a_re * a_re + a_im * a_im
    q_re = ((n_re * a_re + n_im * a_im) / den)[:, None, :]
    q_im = ((n_im * a_re - n_re * a_im) / den)[:, None, :]
    b_re = bre_ref[...]
    b_im = bim_ref[...]
    bb_re = q_re * b_re - q_im * b_im
    bb_im = q_re * b_im + q_im * b_re
    c_re = cre_ref[...]
    c_im = cim_ref[...]
    g, h, p = c_re.shape

    pows = [(jnp.ones_like(l_re), jnp.zeros_like(l_re))]
    for _ in range(SSM_CHUNK):
        pr, pi = pows[-1]
        pows.append((pr * l_re - pi * l_im, pr * l_im + pi * l_re))
    lcre_ref[...] = pows[SSM_CHUNK][0]
    lcim_ref[...] = pows[SSM_CHUNK][1]

    for k in range(SSM_CHUNK):
        pr, pi = pows[SSM_CHUNK - 1 - k]
        pr, pi = pr[:, None, :], pi[:, None, :]
        ere_ref[k] = pr * bb_re - pi * bb_im
        eim_ref[k] = pr * bb_im + pi * bb_re
    for r in range(SSM_CHUNK):
        pr, pi = pows[r + 1]
        pr, pi = pr[:, None, :], pi[:, None, :]
        fre_ref[r] = c_re * pr - c_im * pi
        fim_ref[r] = -(c_re * pi + c_im * pr)
    nt = (((1,), (1,)), ((), ()))
    c_re2 = c_re.reshape(g * h, p)
    c_im2 = c_im.reshape(g * h, p)
    for j in range(SSM_CHUNK):
        pr, pi = pows[j]
        pr, pi = pr[:, None, :], pi[:, None, :]
        w_re = (pr * bb_re - pi * bb_im).reshape(g * h, p)
        w_im = (pr * bb_im + pi * bb_re).reshape(g * h, p)
        kk = (lax.dot_general(w_re, c_re2, nt, precision=lax.Precision.HIGHEST,
                              preferred_element_type=F32)
              - lax.dot_general(w_im, c_im2, nt, precision=lax.Precision.HIGHEST,
                                preferred_element_type=F32))
        for q in range(g * h // QUAD_CH):
            kq_ref[j, q] = kk[q * QUAD_CH:(q + 1) * QUAD_CH, q * QUAD_CH:(q + 1) * QUAD_CH]


def _ssm_prep(a_re, a_im, log_dt, b_re_t, b_im_t, c_re, c_im):
    g, p = a_re.shape
    h = b_re_t.shape[1]
    n = SSM_CHUNK
    return pl.pallas_call(
        _ssm_prep_kernel,
        out_shape=(jax.ShapeDtypeStruct((g, p), F32), jax.ShapeDtypeStruct((g, p), F32),
                   jax.ShapeDtypeStruct((n, g, h, p), F32), jax.ShapeDtypeStruct((n, g, h, p), F32),
                   jax.ShapeDtypeStruct((n, g, h, p), F32), jax.ShapeDtypeStruct((n, g, h, p), F32),
                   jax.ShapeDtypeStruct((n, g * h // QUAD_CH, QUAD_CH, QUAD_CH), F32)),
        name="ssm_prep",
    )(a_re, a_im, log_dt.reshape(g, 1), b_re_t, b_im_t, c_re, c_im)


def _mem_kv_kernel(mem_ref, g_ref, wkt_ref, wv_ref, kt_ref, v_ref):
    mn = _rms(mem_ref[...], g_ref[...]).astype(BF16)
    kt = lax.dot_general(wkt_ref[...], mn, (((1,), (1,)), ((), ())),
                         preferred_element_type=F32)
    kt_ref[...] = kt.astype(BF16)
    v_ref[...] = _dot(mn, wv_ref[...]).astype(BF16)


def _mem_kv(mem, g, wkt, wv):
    b, m, d = mem.shape
    w = wv.shape[1]
    return pl.pallas_call(
        _mem_kv_kernel,
        grid=(b,),
        in_specs=[pl.BlockSpec((None, m, d), lambda i: (i, 0, 0)),
                  pl.BlockSpec((1, d), lambda i: (0, 0)),
                  pl.BlockSpec((w, d), lambda i: (0, 0)),
                  pl.BlockSpec((d, w), lambda i: (0, 0))],
        out_specs=(pl.BlockSpec((None, w, m), lambda i: (i, 0, 0)),
                   pl.BlockSpec((None, m, w), lambda i: (i, 0, 0))),
        out_shape=(jax.ShapeDtypeStruct((b, w, m), BF16), jax.ShapeDtypeStruct((b, m, w), BF16)),
        name="mem_kv",
    )(mem, g, wkt, wv)


def _mixer_kernel(x_ref, g1_ref, win_ref, convw_ref, convp_ref, kt_ref, v_ref, attnp_ref,
                  u_ref, sgs_ref, part_ref, cv_ref, *, cw, sw, mw, d):
    i = pl.program_id(1)
    tb = x_ref.shape[0]
    h = _rms(x_ref[...], g1_ref[...]).astype(BF16)

    def proj(lo, hi):
        return _dot(h, win_ref[:, lo:hi])

    o_u = 3 * cw
    o_q = o_u + sw
    o_gc = o_q + mw
    o_gs = o_gc + d
    o_gm = o_gs + d

    q = proj(o_q, o_gc)
    scale = MEM_HEAD_DIM ** -0.5
    scores = []
    for hd in range(MEM_HEADS):
        lo = hd * MEM_HEAD_DIM
        scores.append(_dot(q[:, lo:lo + MEM_HEAD_DIM].astype(BF16),
                           kt_ref[lo:lo + MEM_HEAD_DIM, :]) * scale)
    c_gate = proj(cw, 2 * cw)
    v_conv = proj(2 * cw, 3 * cw)

    heads = []
    for hd in range(MEM_HEADS):
        lo = hd * MEM_HEAD_DIM
        s = scores[hd]
        e = jnp.exp(s - jnp.max(s, axis=-1, keepdims=True))
        pv = _dot(e.astype(BF16), v_ref[:, lo:lo + MEM_HEAD_DIM])
        heads.append(pv / jnp.sum(e, axis=-1, keepdims=True))
    o = jnp.concatenate(heads, axis=-1).astype(BF16)
    b_gate = proj(0, cw)

    cv = c_gate * v_conv

    @pl.when(i == 0)
    def _():
        cv_ref[0:SUBLANES, :] = jnp.zeros((SUBLANES, cw), F32)

    cv_ref[SUBLANES:SUBLANES + tb, :] = cv
    conv = (convw_ref[0:1, :] * cv_ref[SUBLANES - 2:SUBLANES - 2 + tb, :]
            + convw_ref[1:2, :] * cv_ref[SUBLANES - 1:SUBLANES - 1 + tb, :]
            + convw_ref[2:3, :] * cv)
    cv_ref[0:SUBLANES, :] = cv_ref[tb:tb + SUBLANES, :]
    pre = (b_gate * conv).astype(BF16)

    g_c = proj(o_gc, o_gs)
    y_mem = _dot(o, attnp_ref[...])
    y_conv = _dot(pre, convp_ref[...])
    g_m = proj(o_gm, o_gm + d)
    part_ref[...] = (jax.nn.sigmoid(g_c) * y_conv + jax.nn.sigmoid(g_m) * y_mem).astype(BF16)
    sgs_ref[...] = jax.nn.sigmoid(proj(o_gs, o_gm)).astype(BF16)
    u_ref[...] = proj(o_u, o_q).astype(BF16)


def _mixer(x, g1, w_in, conv_w, conv_p, kt, v, attn_p, *, cw, sw, mw):
    b, s, d = x.shape
    tb = MIXER_ROW_TILE
    m = kt.shape[2]
    kern = functools.partial(_mixer_kernel, cw=cw, sw=sw, mw=mw, d=d)
    return pl.pallas_call(
        kern,
        grid=(b, s // tb),
        in_specs=[pl.BlockSpec((None, tb, d), lambda bi, i: (bi, i, 0)),
                  _const_spec((1, d)),
                  _const_spec(w_in.shape),
                  _const_spec(conv_w.shape),
                  _const_spec(conv_p.shape),
                  pl.BlockSpec((None, mw, m), lambda bi, i: (bi, 0, 0)),
                  pl.BlockSpec((None, m, mw), lambda bi, i: (bi, 0, 0)),
                  _const_spec(attn_p.shape)],
        out_specs=(pl.BlockSpec((None, tb, sw), lambda bi, i: (bi, i, 0)),
                   pl.BlockSpec((None, tb, d), lambda bi, i: (bi, i, 0)),
                   pl.BlockSpec((None, tb, d), lambda bi, i: (bi, i, 0))),
        out_shape=(jax.ShapeDtypeStruct((b, s, sw), BF16),
                   jax.ShapeDtypeStruct((b, s, d), BF16),
                   jax.ShapeDtypeStruct((b, s, d), BF16)),
        scratch_shapes=[pltpu.VMEM((tb + SUBLANES, cw), F32)],
        compiler_params=pltpu.CompilerParams(
            dimension_semantics=("arbitrary", "arbitrary"),
            vmem_limit_bytes=VMEM_LIMIT_BYTES),
        name="mixer",
    )(x, g1, w_in, conv_w, conv_p, kt, v, attn_p)


def _s5_kernel(u_ref, ere_ref, eim_ref, lcre_ref, lcim_ref, fre_ref, fim_ref, tz_ref,
               gw_ref, gb_ref, sp_ref, y_ref,
               utm_ref, x_ref, st_ref, hst_ref, yc_ref, ytm_ref, *, n_state):
    batch, n_t, sw = u_ref.shape
    n_c = n_t // SSM_CHUNK
    n_q = sw // QUAD_CH
    tile = SSM_CHUNK * QUAD_CH
    half = LANES // 2
    lo = lax.broadcasted_iota(jnp.int32, (batch, LANES), 1) < half

    @pl.when(pl.program_id(0) == 0)
    def _():
        hst_ref[...] = jnp.zeros_like(hst_ref)

    utm_ref[...] = jnp.swapaxes(u_ref[...].astype(F32), 0, 1).reshape(n_t * batch, sw)

    def gather_chunk(c, carry):
        r0 = pl.multiple_of(c * batch, batch)
        for lc in range(sw // LANES):
            for kp in range(SSM_CHUNK // 2):
                t0 = pl.multiple_of((c * SSM_CHUNK + 2 * kp) * batch, batch)
                s0 = utm_ref[pl.ds(t0, batch), pl.ds(lc * LANES, LANES)]
                s1 = utm_ref[pl.ds(t0 + batch, batch), pl.ds(lc * LANES, LANES)]
                col = kp * LANES
                x_ref[pl.ds(r0, batch), pl.ds((2 * lc) * tile + col, LANES)] = jnp.where(
                    lo, s0, pltpu.roll(s1, half, 1))
                x_ref[pl.ds(r0, batch), pl.ds((2 * lc + 1) * tile + col, LANES)] = jnp.where(
                    lo, pltpu.roll(s0, half, 1), s1)
        return carry

    lax.fori_loop(0, n_c, gather_chunk, 0, unroll=SHUFFLE_UNROLL)

    for q in range(n_q):
        xq = x_ref[:, q * tile:(q + 1) * tile].astype(BF16)
        st_ref[:, q * tile:(q + 1) * tile] = _dot(xq, ere_ref[q])
        st_ref[:, n_state + q * tile:n_state + (q + 1) * tile] = _dot(xq, eim_ref[q])

    for cc in range(n_state // SCAN_COLS):
        re = slice(cc * SCAN_COLS, (cc + 1) * SCAN_COLS)
        im = slice(n_state + cc * SCAN_COLS, n_state + (cc + 1) * SCAN_COLS)
        l_re = lcre_ref[:, re]
        l_im = lcim_ref[:, re]

        def step(c, carry):
            h_re, h_im = carry
            r0 = pl.multiple_of(c * batch, batch)
            b_re = st_ref[pl.ds(r0, batch), re]
            b_im = st_ref[pl.ds(r0, batch), im]
            st_ref[pl.ds(r0, batch), re] = h_re
            st_ref[pl.ds(r0, batch), im] = h_im
            return (l_re * h_re - l_im * h_im + b_re, l_re * h_im + l_im * h_re + b_im)

        h_re, h_im = lax.fori_loop(0, n_c, step, (hst_ref[:, re], hst_ref[:, im]),
                                   unroll=SCAN_UNROLL)
        hst_ref[:, re] = h_re
        hst_ref[:, im] = h_im

    for q in range(n_q):
        cols = slice(q * tile, (q + 1) * tile)
        xq = x_ref[:, cols].astype(BF16)
        h_re = st_ref[:, cols].astype(BF16)
        h_im = st_ref[:, n_state + q * tile:n_state + (q + 1) * tile].astype(BF16)
        yc_ref[:, cols] = (_dot(h_re, fre_ref[q]) + _dot(h_im, fim_ref[q]) + _dot(xq, tz_ref[q]))

    def scatter_chunk(c, carry):
        r0 = pl.multiple_of(c * batch, batch)
        for j in range(sw // LANES):
            for rp in range(SSM_CHUNK // 2):
                col = rp * LANES
                a = yc_ref[pl.ds(r0, batch), pl.ds((2 * j) * tile + col, LANES)]
                b = yc_ref[pl.ds(r0, batch), pl.ds((2 * j + 1) * tile + col, LANES)]
                t0 = pl.multiple_of((c * SSM_CHUNK + 2 * rp) * batch, batch)
                ytm_ref[pl.ds(t0, batch), pl.ds(j * LANES, LANES)] = jnp.where(
                    lo, a, pltpu.roll(b, half, 1))
                ytm_ref[pl.ds(t0 + batch, batch), pl.ds(j * LANES, LANES)] = jnp.where(
                    lo, pltpu.roll(a, half, 1), b)
        return carry

    lax.fori_loop(0, n_c, scatter_chunk, 0, unroll=SHUFFLE_UNROLL)

    sub_t = POST_ROWS // batch
    for sb in range(n_t // sub_t):
        y = ytm_ref[sb * POST_ROWS:(sb + 1) * POST_ROWS, :]
        y = jax.nn.gelu(y)
        y = y * jax.nn.sigmoid(_dot(y.astype(BF16), gw_ref[...]) + gb_ref[...])
        out = _dot(y.astype(BF16), sp_ref[...]).reshape(sub_t, batch, -1)
        y_ref[:, sb * sub_t:(sb + 1) * sub_t, :] = jnp.swapaxes(out, 0, 1).astype(BF16)


def _s5(u, ere, eim, lcre, lcim, fre, fim, tz, glu_w, glu_b, ssm_p):
    batch, s, sw = u.shape
    d = ssm_p.shape[1]
    n_state = lcre.shape[1]
    n_t = TIME_TILE
    rows_c = (n_t // SSM_CHUNK) * batch
    kern = functools.partial(_s5_kernel, n_state=n_state)
    consts = (ere, eim, lcre, lcim, fre, fim, tz, glu_w, glu_b, ssm_p)
    return pl.pallas_call(
        kern,
        grid=(s // n_t,),
        in_specs=[pl.BlockSpec((batch, n_t, sw), lambda i: (0, i, 0))]
                 + [_const_spec(c.shape) for c in consts],
        out_specs=pl.BlockSpec((batch, n_t, d), lambda i: (0, i, 0)),
        out_shape=jax.ShapeDtypeStruct((batch, s, d), BF16),
        scratch_shapes=[pltpu.VMEM((batch * n_t, sw), F32),
                        pltpu.VMEM((rows_c, SSM_CHUNK * sw), F32),
                        pltpu.VMEM((rows_c, 2 * n_state), F32),
                        pltpu.VMEM((batch, 2 * n_state), F32),
                        pltpu.VMEM((rows_c, SSM_CHUNK * sw), F32),
                        pltpu.VMEM((batch * n_t, sw), F32)],
        compiler_params=pltpu.CompilerParams(
            dimension_semantics=("arbitrary",),
            vmem_limit_bytes=VMEM_LIMIT_BYTES),
        name="s5",
    )(u, *consts)


def _ffn_kernel(x_ref, part_ref, sgs_ref, yssm_ref, wo_ref, g2_ref, wg_ref, wu_ref, wd_ref,
                gf_ref, o_ref, *, chunks):
    tb = x_ref.shape[0] // ROW_SUBTILES
    subs = [slice(sub * tb, (sub + 1) * tb) for sub in range(ROW_SUBTILES)]
    x1s = []
    for rows in subs:
        merged = (part_ref[rows, :].astype(F32)
                  + sgs_ref[rows, :].astype(F32) * yssm_ref[rows, :].astype(F32))
        x1s.append(x_ref[rows, :] + _dot(merged.astype(BF16), wo_ref[...]))
    h2s = [_rms(x1, g2_ref[...]).astype(BF16) for x1 in x1s]
    outs = []
    for x1, h2 in zip(x1s, h2s):
        acc = x1
        for lo, hi in chunks:
            g = _dot(h2, wg_ref[:, lo:hi])
            a = (g * jax.nn.sigmoid(g) * _dot(h2, wu_ref[:, lo:hi])).astype(BF16)
            acc = acc + _dot(a, wd_ref[lo:hi, :])
        outs.append(_rms(acc, gf_ref[...]))
    o_ref[...] = jnp.concatenate(outs, axis=0)


def _ffn_chunks(hidden):
    step = 4 * MXU_TILE
    return tuple((lo, min(lo + step, hidden)) for lo in range(0, hidden, step))


def _ffn(x, part, sgs, yssm, w_o, g2, wg, wu, wd, gf):
    b, s, d = x.shape
    tb = ROW_TILE
    kern = functools.partial(_ffn_kernel, chunks=_ffn_chunks(wg.shape[1]))
    row_spec = pl.BlockSpec((None, tb, d), lambda bi, i: (bi, i, 0))
    return pl.pallas_call(
        kern,
        grid=(b, s // tb),
        in_specs=[row_spec, row_spec, row_spec, row_spec,
                  _const_spec(w_o.shape), _const_spec((1, d)),
                  _const_spec(wg.shape), _const_spec(wu.shape), _const_spec(wd.shape),
                  _const_spec((1, d))],
        out_specs=row_spec,
        out_shape=jax.ShapeDtypeStruct((b, s, d), x.dtype),
        compiler_params=pltpu.CompilerParams(
            dimension_semantics=("arbitrary", "arbitrary"),
            vmem_limit_bytes=VMEM_LIMIT_BYTES),
        name="ffn",
    )(x, part, sgs, yssm, w_o, g2, wg, wu, wd, gf)


def _quad_eye(n):
    return jnp.eye(n, dtype=F32)


def _pack_e(e):
    k, g, h, p = e.shape
    a = QUAD_CH // h
    e5 = e.reshape(k, g // a, a, h, p)
    t = e5[:, :, :, :, None, :] * _quad_eye(a)[None, None, :, None, :, None]
    return t.transpose(1, 0, 2, 3, 4, 5).reshape(g // a, k * a * h, a * p)


def _pack_f(f):
    return _pack_e(f).transpose(0, 2, 1)


def _pack_toeplitz(kq, d_skip, h):
    n, n_q, qc, _ = kq.shape
    idx = jnp.arange(qc)
    same_group = (idx[:, None] // h == idx[None, :] // h).astype(F32)
    blk = kq * same_group
    blk = blk.at[0].add(jnp.eye(qc, dtype=F32) * d_skip.reshape(n_q, 1, qc))
    zero = jnp.zeros_like(blk[0])
    rows = [jnp.concatenate([blk[r - k] if r >= k else zero for r in range(n)], axis=-1)
            for k in range(n)]
    return jnp.concatenate(rows, axis=1)


def kernel(x, mem, norm1_g, w_in, conv_w, conv_proj, ssm_A_re, ssm_A_im, ssm_log_dt, ssm_B_re,
           ssm_B_im, ssm_C_re, ssm_C_im, ssm_D, ssm_glu_w, ssm_glu_b, ssm_proj, mem_norm_g,
           attn_wk, attn_wv, attn_proj, w_o, norm2_g, ffn_w_gate, ffn_w_up, ffn_w_down,
           final_norm_g):
    b, s, d = x.shape
    depth = w_in.shape[0]
    cw = conv_w.shape[2]
    sw = ssm_D.shape[1]
    mw = attn_wk.shape[2]
    n_groups = ssm_A_re.shape[1]
    n_state = n_groups * SSM_STATE
    assert depth == 1, "the final norm is fused into the single layer's ffn kernel"
    assert b == SUBLANES, "the S5 recurrence keeps the batch on the sublane axis"

    for l in range(depth):
        lc_re, lc_im, e_re, e_im, f_re, f_im, kq = _ssm_prep(
            ssm_A_re[l], ssm_A_im[l], ssm_log_dt[l],
            ssm_B_re[l].transpose(0, 2, 1), ssm_B_im[l].transpose(0, 2, 1),
            ssm_C_re[l], ssm_C_im[l])
        ere = _pack_e(e_re).astype(BF16)
        eim = _pack_e(e_im).astype(BF16)
        fre = _pack_f(f_re).astype(BF16)
        fim = _pack_f(f_im).astype(BF16)
        tz = _pack_toeplitz(kq, ssm_D[l], SSM_GROUP).astype(BF16)
        lcre = jnp.broadcast_to(lc_re.reshape(1, n_state), (b, n_state))
        lcim = jnp.broadcast_to(lc_im.reshape(1, n_state), (b, n_state))

        kt, v = _mem_kv(mem, mem_norm_g[l].reshape(1, d), attn_wk[l].T.astype(BF16),
                        attn_wv[l].astype(BF16))
        u, sgs, part = _mixer(
            x, norm1_g[l].reshape(1, d), w_in[l].astype(BF16), conv_w[l],
            conv_proj[l].astype(BF16), kt, v, attn_proj[l].astype(BF16), cw=cw, sw=sw, mw=mw)
        yssm = _s5(u, ere, eim, lcre, lcim, fre, fim, tz,
                   ssm_glu_w[l].astype(BF16), ssm_glu_b[l].reshape(1, sw),
                   ssm_proj[l].astype(BF16))
        x = _ffn(x, part, sgs, yssm, w_o[l].astype(BF16),
                 norm2_g[l].reshape(1, d), ffn_w_gate[l].astype(BF16), ffn_w_up[l].astype(BF16),
                 ffn_w_down[l].astype(BF16), final_norm_g.reshape(1, d))
    return x
```

```python
import functools
import math

import jax
import jax.numpy as jnp
from jax import lax
from jax.experimental import pallas as pl
from jax.experimental.pallas import tpu as pltpu

F32 = jnp.float32
BF16 = jnp.bfloat16

EPS = 1e-6
CONV_K = 3
SSM_GROUP = 16
SSM_STATE = 64
MEM_HEADS = 4
MEM_HEAD_DIM = 128

MXU_TILE = 256
SUBLANES = 8
LANES = 128
VMEM_LIMIT_BYTES = 56 * 1024 * 1024

MIXER_ROW_TILE = 1024
ROW_TILE = 512
ROW_SUBTILES = 2
TIME_TILE = 256
SSM_CHUNK = 4
QUAD_CH = MXU_TILE // SSM_CHUNK
SCAN_COLS = 512
POST_ROWS = 512


def _rms(x, g):
    ms = jnp.mean(x * x, axis=-1, keepdims=True)
    return x * lax.rsqrt(ms + EPS) * g


def _dot(a, b):
    return jnp.dot(a, b, preferred_element_type=F32)


def _const_spec(shape):
    nd = len(shape)
    return pl.BlockSpec(shape, lambda *_: (0,) * nd, pipeline_mode=pl.Buffered(1))


def _ssm_prep_kernel(are_ref, aim_ref, ldt_ref, bre_ref, bim_ref, cre_ref, cim_ref,
                     lcre_ref, lcim_ref, ere_ref, eim_ref, fre_ref, fim_ref, kq_ref):
    a_re = are_ref[...]
    a_im = aim_ref[...]
    dt = jnp.exp(ldt_ref[...])
    mag = jnp.exp(a_re * dt)
    l_re = mag * jnp.cos(a_im * dt)
    l_im = mag * jnp.sin(a_im * dt)
    n_re = l_re - 1.0
    n_im = l_im
    den = a_re * a_re + a_im * a_im
    q_re = ((n_re * a_re + n_im * a_im) / den)[:, None, :]
    q_im = ((n_im * a_re - n_re * a_im) / den)[:, None, :]
    b_re = bre_ref[...]
    b_im = bim_ref[...]
    bb_re = q_re * b_re - q_im * b_im
    bb_im = q_re * b_im + q_im * b_re
    c_re = cre_ref[...]
    c_im = cim_ref[...]
    g, h, p = c_re.shape

    pows = [(jnp.ones_like(l_re), jnp.zeros_like(l_re))]
    for _ in range(SSM_CHUNK):
        pr, pi = pows[-1]
        pows.append((pr * l_re - pi * l_im, pr * l_im + pi * l_re))
    lcre_ref[...] = pows[SSM_CHUNK][0]
    lcim_ref[...] = pows[SSM_CHUNK][1]

    for k in range(SSM_CHUNK):
        pr, pi = pows[SSM_CHUNK - 1 - k]
        pr, pi = pr[:, None, :], pi[:, None, :]
        ere_ref[k] = pr * bb_re - pi * bb_im
        eim_ref[k] = pr * bb_im + pi * bb_re
    for r in range(SSM_CHUNK):
        pr, pi = pows[r + 1]
        pr, pi = pr[:, None, :], pi[:, None, :]
        fre_ref[r] = c_re * pr - c_im * pi
        fim_ref[r] = -(c_re * pi + c_im * pr)
    nt = (((1,), (1,)), ((), ()))
    c_re2 = c_re.reshape(g * h, p)
    c_im2 = c_im.reshape(g * h, p)
    for j in range(SSM_CHUNK):
        pr, pi = pows[j]
        pr, pi = pr[:, None, :], pi[:, None, :]
        w_re = (pr * bb_re - pi * bb_im).reshape(g * h, p)
        w_im = (pr * bb_im + pi * bb_re).reshape(g * h, p)
        kk = (lax.dot_general(w_re, c_re2, nt, precision=lax.Precision.HIGHEST,
                              preferred_element_type=F32)
              - lax.dot_general(w_im, c_im2, nt, precision=lax.Precision.HIGHEST,
                                preferred_element_type=F32))
        for q in range(g * h // QUAD_CH):
            kq_ref[j, q] = kk[q * QUAD_CH:(q + 1) * QUAD_CH, q * QUAD_CH:(q + 1) * QUAD_CH]


def _ssm_prep(a_re, a_im, log_dt, b_re_t, b_im_t, c_re, c_im):
    g, p = a_re.shape
    h = b_re_t.shape[1]
    n = SSM_CHUNK
    return pl.pallas_call(
        _ssm_prep_kernel,
        out_shape=(jax.ShapeDtypeStruct((g, p), F32), jax.ShapeDtypeStruct((g, p), F32),
                   jax.ShapeDtypeStruct((n, g, h, p), F32), jax.ShapeDtypeStruct((n, g, h, p), F32),
                   jax.ShapeDtypeStruct((n, g, h, p), F32), jax.ShapeDtypeStruct((n, g, h, p), F32),
                   jax.ShapeDtypeStruct((n, g * h // QUAD_CH, QUAD_CH, QUAD_CH), F32)),
        name="ssm_prep",
    )(a_re, a_im, log_dt.reshape(g, 1), b_re_t, b_im_t, c_re, c_im)


def _mem_kv_kernel(mem_ref, g_ref, wkt_ref, wv_ref, kt_ref, v_ref):
    mn = _rms(mem_ref[...], g_ref[...]).astype(BF16)
    kt = lax.dot_general(wkt_ref[...], mn, (((1,), (1,)), ((), ())),
                         preferred_element_type=F32)
    kt_ref[...] = kt.astype(BF16)
    v_ref[...] = _dot(mn, wv_ref[...]).astype(BF16)


def _mem_kv(mem, g, wkt, wv):
    b, m, d = mem.shape
    w = wv.shape[1]
    return pl.pallas_call(
        _mem_kv_kernel,
        grid=(b,),
        in_specs=[pl.BlockSpec((None, m, d), lambda i: (i, 0, 0)),
                  pl.BlockSpec((1, d), lambda i: (0, 0)),
                  pl.BlockSpec((w, d), lambda i: (0, 0)),
                  pl.BlockSpec((d, w), lambda i: (0, 0))],
        out_specs=(pl.BlockSpec((None, w, m), lambda i: (i, 0, 0)),
                   pl.BlockSpec((None, m, w), lambda i: (i, 0, 0))),
        out_shape=(jax.ShapeDtypeStruct((b, w, m), BF16), jax.ShapeDtypeStruct((b, m, w), BF16)),
        name="mem_kv",
    )(mem, g, wkt, wv)


def _mixer_kernel(x_ref, g1_ref, win_ref, convw_ref, convp_ref, kt_ref, v_ref, attnp_ref,
                  u_ref, sgs_ref, part_ref, cv_ref, *, cw, sw, mw, d):
    i = pl.program_id(1)
    tb = x_ref.shape[0]
    h = _rms(x_ref[...], g1_ref[...]).astype(BF16)

    def proj(lo, hi):
        return _dot(h, win_ref[:, lo:hi])

    o_u = 3 * cw
    o_q = o_u + sw
    o_gc = o_q + mw
    o_gs = o_gc + d
    o_gm = o_gs + d

    q = proj(o_q, o_gc)
    scale = MEM_HEAD_DIM ** -0.5
    scores = []
    for hd in range(MEM_HEADS):
        lo = hd * MEM_HEAD_DIM
        scores.append(_dot(q[:, lo:lo + MEM_HEAD_DIM].astype(BF16),
                           kt_ref[lo:lo + MEM_HEAD_DIM, :]) * scale)
    c_gate = proj(cw, 2 * cw)
    v_conv = proj(2 * cw, 3 * cw)

    heads = []
    for hd in range(MEM_HEADS):
        lo = hd * MEM_HEAD_DIM
        s = scores[hd]
        e = jnp.exp(s - jnp.max(s, axis=-1, keepdims=True))
        pv = _dot(e.astype(BF16), v_ref[:, lo:lo + MEM_HEAD_DIM])
        heads.append(pv / jnp.sum(e, axis=-1, keepdims=True))
    o = jnp.concatenate(heads, axis=-1).astype(BF16)
    b_gate = proj(0, cw)

    cv = c_gate * v_conv

    @pl.when(i == 0)
    def _():
        cv_ref[0:SUBLANES, :] = jnp.zeros((SUBLANES, cw), F32)

    cv_ref[SUBLANES:SUBLANES + tb, :] = cv
    conv = (convw_ref[0:1, :] * cv_ref[SUBLANES - 2:SUBLANES - 2 + tb, :]
            + convw_ref[1:2, :] * cv_ref[SUBLANES - 1:SUBLANES - 1 + tb, :]
            + convw_ref[2:3, :] * cv)
    cv_ref[0:SUBLANES, :] = cv_ref[tb:tb + SUBLANES, :]
    pre = (b_gate * conv).astype(BF16)

    g_c = proj(o_gc, o_gs)
    y_mem = _dot(o, attnp_ref[...])
    y_conv = _dot(pre, convp_ref[...])
    g_m = proj(o_gm, o_gm + d)
    part_ref[...] = (jax.nn.sigmoid(g_c) * y_conv + jax.nn.sigmoid(g_m) * y_mem).astype(BF16)
    sgs_ref[...] = jax.nn.sigmoid(proj(o_gs, o_gm)).astype(BF16)
    u_ref[...] = proj(o_u, o_q).astype(BF16)


def _mixer(x, g1, w_in, conv_w, conv_p, kt, v, attn_p, *, cw, sw, mw):
    b, s, d = x.shape
    tb = MIXER_ROW_TILE
    m = kt.shape[2]
    kern = functools.partial(_mixer_kernel, cw=cw, sw=sw, mw=mw, d=d)
    return pl.pallas_call(
        kern,
        grid=(b, s // tb),
        in_specs=[pl.BlockSpec((None, tb, d), lambda bi, i: (bi, i, 0)),
                  _const_spec((1, d)),
                  _const_spec(w_in.shape),
                  _const_spec(conv_w.shape),
                  _const_spec(conv_p.shape),
                  pl.BlockSpec((None, mw, m), lambda bi, i: (bi, 0, 0)),
                  pl.BlockSpec((None, m, mw), lambda bi, i: (bi, 0, 0)),
                  _const_spec(attn_p.shape)],
        out_specs=(pl.BlockSpec((None, tb, sw), lambda bi, i: (bi, i, 0)),
                   pl.BlockSpec((None, tb, d), lambda bi, i: (bi, i, 0)),
                   pl.BlockSpec((None, tb, d), lambda bi, i: (bi, i, 0))),
        out_shape=(jax.ShapeDtypeStruct((b, s, sw), BF16),
                   jax.ShapeDtypeStruct((b, s, d), BF16),
                   jax.ShapeDtypeStruct((b, s, d), BF16)),
        scratch_shapes=[pltpu.VMEM((tb + SUBLANES, cw), F32)],
        compiler_params=pltpu.CompilerParams(
            dimension_semantics=("arbitrary", "arbitrary"),
            vmem_limit_bytes=VMEM_LIMIT_BYTES),
        name="mixer",
    )(x, g1, w_in, conv_w, conv_p, kt, v, attn_p)


def _s5_kernel(u_ref, ere_ref, eim_ref, lcre_ref, lcim_ref, fre_ref, fim_ref, tz_ref,
               gw_ref, gb_ref, sp_ref, y_ref,
               utm_ref, x_ref, st_ref, hst_ref, yc_ref, ytm_ref, *, n_state):
    batch, n_t, sw = u_ref.shape
    n_c = n_t // SSM_CHUNK
    n_q = sw // QUAD_CH
    tile = SSM_CHUNK * QUAD_CH
    half = LANES // 2
    lo = lax.broadcasted_iota(jnp.int32, (batch, LANES), 1) < half
    step_i = pl.program_id(0)
    slot_a = step_i % 2
    slot_b = 1 - slot_a

    @pl.when(step_i == 0)
    def _():
        hst_ref[...] = jnp.zeros_like(hst_ref)
        ytm_ref[1] = jnp.zeros(ytm_ref.shape[1:], F32)

    sub_t = POST_ROWS // batch

    def post(sb):
        y = ytm_ref[slot_b, sb * POST_ROWS:(sb + 1) * POST_ROWS, :]
        y = jax.nn.gelu(y)
        y = y * jax.nn.sigmoid(_dot(y.astype(BF16), gw_ref[...]) + gb_ref[...])
        out = _dot(y.astype(BF16), sp_ref[...]).reshape(sub_t, batch, -1)
        y_ref[:, sb * sub_t:(sb + 1) * sub_t, :] = jnp.swapaxes(out, 0, 1).astype(BF16)

    assert n_t // sub_t == 4, "stage B is interleaved as four sub-blocks"

    utm_ref[...] = jnp.swapaxes(u_ref[...].astype(F32), 0, 1).reshape(n_t * batch, sw)
    post(0)
    for c in range(n_c):
        r0 = c * batch
        for lc in range(sw // LANES):
            for kp in range(SSM_CHUNK // 2):
                t0 = (c * SSM_CHUNK + 2 * kp) * batch
                s0 = utm_ref[t0:t0 + batch, lc * LANES:(lc + 1) * LANES]
                s1 = utm_ref[t0 + batch:t0 + 2 * batch, lc * LANES:(lc + 1) * LANES]
                col = (2 * lc) * tile + kp * LANES
                x_ref[r0:r0 + batch, col:col + LANES] = jnp.where(lo, s0, pltpu.roll(s1, half, 1))
                col += tile
                x_ref[r0:r0 + batch, col:col + LANES] = jnp.where(lo, pltpu.roll(s0, half, 1), s1)

    for q in range(n_q):
        xq = x_ref[:, q * tile:(q + 1) * tile].astype(BF16)
        st_ref[:, q * tile:(q + 1) * tile] = _dot(xq, ere_ref[q])
        st_ref[:, n_state + q * tile:n_state + (q + 1) * tile] = _dot(xq, eim_ref[q])
    post(1)

    for cc in range(n_state // SCAN_COLS):
        re = slice(cc * SCAN_COLS, (cc + 1) * SCAN_COLS)
        im = slice(n_state + cc * SCAN_COLS, n_state + (cc + 1) * SCAN_COLS)
        l_re = lcre_ref[:, re]
        l_im = lcim_ref[:, re]
        h_re = hst_ref[:, re]
        h_im = hst_ref[:, im]
        for c in range(n_c):
            rows = slice(c * batch, (c + 1) * batch)
            b_re = st_ref[rows, re]
            b_im = st_ref[rows, im]
            st_ref[rows, re] = h_re
            st_ref[rows, im] = h_im
            h_re, h_im = (l_re * h_re - l_im * h_im + b_re, l_re * h_im + l_im * h_re + b_im)
        hst_ref[:, re] = h_re
        hst_ref[:, im] = h_im
    post(2)

    for q in range(n_q):
        cols = slice(q * tile, (q + 1) * tile)
        xq = x_ref[:, cols].astype(BF16)
        h_re = st_ref[:, cols].astype(BF16)
        h_im = st_ref[:, n_state + q * tile:n_state + (q + 1) * tile].astype(BF16)
        yc_ref[:, cols] = (_dot(h_re, fre_ref[q]) + _dot(h_im, fim_ref[q]) + _dot(xq, tz_ref[q]))
    post(3)

    for c in range(n_c):
        r0 = c * batch
        for j in range(sw // LANES):
            for rp in range(SSM_CHUNK // 2):
                col = (2 * j) * tile + rp * LANES
                a = yc_ref[r0:r0 + batch, col:col + LANES]
                b = yc_ref[r0:r0 + batch, col + tile:col + tile + LANES]
                t0 = (c * SSM_CHUNK + 2 * rp) * batch
                ytm_ref[slot_a, t0:t0 + batch, j * LANES:(j + 1) * LANES] = jnp.where(
                    lo, a, pltpu.roll(b, half, 1))
                ytm_ref[slot_a, t0 + batch:t0 + 2 * batch, j * LANES:(j + 1) * LANES] = jnp.where(
                    lo, pltpu.roll(a, half, 1), b)


def _s5(u, ere, eim, lcre, lcim, fre, fim, tz, glu_w, glu_b, ssm_p):
    batch, s, sw = u.shape
    d = ssm_p.shape[1]
    n_state = lcre.shape[1]
    n_t = TIME_TILE
    n_blocks = s // n_t
    rows_c = (n_t // SSM_CHUNK) * batch
    kern = functools.partial(_s5_kernel, n_state=n_state)
    consts = (ere, eim, lcre, lcim, fre, fim, tz, glu_w, glu_b, ssm_p)
    return pl.pallas_call(
        kern,
        grid=(n_blocks + 1,),
        in_specs=[pl.BlockSpec((batch, n_t, sw), lambda i: (0, jnp.minimum(i, n_blocks - 1), 0))]
                 + [_const_spec(c.shape) for c in consts],
        out_specs=pl.BlockSpec((batch, n_t, d), lambda i: (0, jnp.maximum(i - 1, 0), 0)),
        out_shape=jax.ShapeDtypeStruct((batch, s, d), BF16),
        scratch_shapes=[pltpu.VMEM((batch * n_t, sw), F32),
                        pltpu.VMEM((rows_c, SSM_CHUNK * sw), F32),
                        pltpu.VMEM((rows_c, 2 * n_state), F32),
                        pltpu.VMEM((batch, 2 * n_state), F32),
                        pltpu.VMEM((rows_c, SSM_CHUNK * sw), F32),
                        pltpu.VMEM((2, batch * n_t, sw), F32)],
        compiler_params=pltpu.CompilerParams(
            dimension_semantics=("arbitrary",),
            vmem_limit_bytes=VMEM_LIMIT_BYTES),
        name="s5",
    )(u, *consts)


def _ffn_kernel(x_ref, part_ref, sgs_ref, yssm_ref, wo_ref, g2_ref, wg_ref, wu_ref, wd_ref,
                gf_ref, o_ref, *, chunks):
    tb = x_ref.shape[0] // ROW_SUBTILES
    subs = [slice(sub * tb, (sub + 1) * tb) for sub in range(ROW_SUBTILES)]
    x1s = []
    for rows in subs:
        merged = (part_ref[rows, :].astype(F32)
                  + sgs_ref[rows, :].astype(F32) * yssm_ref[rows, :].astype(F32))
        x1s.append(x_ref[rows, :] + _dot(merged.astype(BF16), wo_ref[...]))
    h2s = [_rms(x1, g2_ref[...]).astype(BF16) for x1 in x1s]
    outs = []
    for x1, h2 in zip(x1s, h2s):
        acc = x1
        for lo, hi in chunks:
            g = _dot(h2, wg_ref[:, lo:hi])
            a = (g * jax.nn.sigmoid(g) * _dot(h2, wu_ref[:, lo:hi])).astype(BF16)
            acc = acc + _dot(a, wd_ref[lo:hi, :])
        outs.append(_rms(acc, gf_ref[...]))
    o_ref[...] = jnp.concatenate(outs, axis=0)


def _ffn_chunks(hidden):
    step = 4 * MXU_TILE
    return tuple((lo, min(lo + step, hidden)) for lo in range(0, hidden, step))


def _ffn(x, part, sgs, yssm, w_o, g2, wg, wu, wd, gf):
    b, s, d = x.shape
    tb = ROW_TILE
    kern = functools.partial(_ffn_kernel, chunks=_ffn_chunks(wg.shape[1]))
    row_spec = pl.BlockSpec((None, tb, d), lambda bi, i: (bi, i, 0))
    return pl.pallas_call(
        kern,
        grid=(b, s // tb),
        in_specs=[row_spec, row_spec, row_spec, row_spec,
                  _const_spec(w_o.shape), _const_spec((1, d)),
                  _const_spec(wg.shape), _const_spec(wu.shape), _const_spec(wd.shape),
                  _const_spec((1, d))],
        out_specs=row_spec,
        out_shape=jax.ShapeDtypeStruct((b, s, d), x.dtype),
        compiler_params=pltpu.CompilerParams(
            dimension_semantics=("arbitrary", "arbitrary"),
            vmem_limit_bytes=VMEM_LIMIT_BYTES),
        name="ffn",
    )(x, part, sgs, yssm, w_o, g2, wg, wu, wd, gf)


def _quad_eye(n):
    return jnp.eye(n, dtype=F32)


def _pack_e(e):
    k, g, h, p = e.shape
    a = QUAD_CH // h
    e5 = e.reshape(k, g // a, a, h, p)
    t = e5[:, :, :, :, None, :] * _quad_eye(a)[None, None, :, None, :, None]
    return t.transpose(1, 0, 2, 3, 4, 5).reshape(g // a, k * a * h, a * p)


def _pack_f(f):
    return _pack_e(f).transpose(0, 2, 1)


def _pack_toeplitz(kq, d_skip, h):
    n, n_q, qc, _ = kq.shape
    idx = jnp.arange(qc)
    same_group = (idx[:, None] // h == idx[None, :] // h).astype(F32)
    blk = kq * same_group
    blk = blk.at[0].add(jnp.eye(qc, dtype=F32) * d_skip.reshape(n_q, 1, qc))
    zero = jnp.zeros_like(blk[0])
    rows = [jnp.concatenate([blk[r - k] if r >= k else zero for r in range(n)], axis=-1)
            for k in range(n)]
    return jnp.concatenate(rows, axis=1)


def kernel(x, mem, norm1_g, w_in, conv_w, conv_proj, ssm_A_re, ssm_A_im, ssm_log_dt, ssm_B_re,
           ssm_B_im, ssm_C_re, ssm_C_im, ssm_D, ssm_glu_w, ssm_glu_b, ssm_proj, mem_norm_g,
           attn_wk, attn_wv, attn_proj, w_o, norm2_g, ffn_w_gate, ffn_w_up, ffn_w_down,
           final_norm_g):
    b, s, d = x.shape
    depth = w_in.shape[0]
    cw = conv_w.shape[2]
    sw = ssm_D.shape[1]
    mw = attn_wk.shape[2]
    n_groups = ssm_A_re.shape[1]
    n_state = n_groups * SSM_STATE
    assert depth == 1, "the final norm is fused into the single layer's ffn kernel"
    assert b == SUBLANES, "the S5 recurrence keeps the batch on the sublane axis"

    for l in range(depth):
        lc_re, lc_im, e_re, e_im, f_re, f_im, kq = _ssm_prep(
            ssm_A_re[l], ssm_A_im[l], ssm_log_dt[l],
            ssm_B_re[l].transpose(0, 2, 1), ssm_B_im[l].transpose(0, 2, 1),
            ssm_C_re[l], ssm_C_im[l])
        ere = _pack_e(e_re).astype(BF16)
        eim = _pack_e(e_im).astype(BF16)
        fre = _pack_f(f_re).astype(BF16)
        fim = _pack_f(f_im).astype(BF16)
        tz = _pack_toeplitz(kq, ssm_D[l], SSM_GROUP).astype(BF16)
        lcre = jnp.broadcast_to(lc_re.reshape(1, n_state), (b, n_state))
        lcim = jnp.broadcast_to(lc_im.reshape(1, n_state), (b, n_state))

        kt, v = _mem_kv(mem, mem_norm_g[l].reshape(1, d), attn_wk[l].T.astype(BF16),
                        attn_wv[l].astype(BF16))
        u, sgs, part = _mixer(
            x, norm1_g[l].reshape(1, d), w_in[l].astype(BF16), conv_w[l],
            conv_proj[l].astype(BF16), kt, v, attn_proj[l].astype(BF16), cw=cw, sw=sw, mw=mw)
        yssm = _s5(u, ere, eim, lcre, lcim, fre, fim, tz,
                   ssm_glu_w[l].astype(BF16), ssm_glu_b[l].reshape(1, sw),
                   ssm_proj[l].astype(BF16))
        x = _ffn(x, part, sgs, yssm, w_o[l].astype(BF16),
                 norm2_g[l].reshape(1, d), ffn_w_gate[l].astype(BF16), ffn_w_up[l].astype(BF16),
                 ffn_w_down[l].astype(BF16), final_norm_g.reshape(1, d))
    return x
```

```python
import functools
import math

import jax
import jax.numpy as jnp
from jax import lax
from jax.experimental import pallas as pl
from jax.experimental.pallas import tpu as pltpu

F32 = jnp.float32
BF16 = jnp.bfloat16

EPS = 1e-6
CONV_K = 3
SSM_GROUP = 16
SSM_STATE = 64
MEM_HEADS = 4
MEM_HEAD_DIM = 128

MXU_TILE = 256
SUBLANES = 8
LANES = 128
VMEM_LIMIT_BYTES = 56 * 1024 * 1024
FFN_VMEM_LIMIT_BYTES = 62 * 1024 * 1024

MIXER_ROW_TILE = 1024
ROW_TILE = 1024
ROW_SUBTILES = 4
TIME_TILE = 256
SSM_CHUNK = 4
QUAD_CH = MXU_TILE // SSM_CHUNK
SCAN_COLS = 512
POST_ROWS = 512


def _rms(x, g):
    ms = jnp.mean(x * x, axis=-1, keepdims=True)
    return x * lax.rsqrt(ms + EPS) * g


def _dot(a, b):
    return jnp.dot(a, b, preferred_element_type=F32)


def _const_spec(shape):
    nd = len(shape)
    return pl.BlockSpec(shape, lambda *_: (0,) * nd, pipeline_mode=pl.Buffered(1))


def _ssm_prep_kernel(are_ref, aim_ref, ldt_ref, bre_ref, bim_ref, cre_ref, cim_ref,
                     lcre_ref, lcim_ref, ere_ref, eim_ref, fre_ref, fim_ref, kq_ref):
    a_re = are_ref[...]
    a_im = aim_ref[...]
    dt = jnp.exp(ldt_ref[...])
    mag = jnp.exp(a_re * dt)
    l_re = mag * jnp.cos(a_im * dt)
    l_im = mag * jnp.sin(a_im * dt)
    n_re = l_re - 1.0
    n_im = l_im
    den = a_re * a_re + a_im * a_im
    q_re = ((n_re * a_re + n_im * a_im) / den)[:, None, :]
    q_im = ((n_im * a_re - n_re * a_im) / den)[:, None, :]
    b_re = bre_ref[...]
    b_im = bim_ref[...]
    bb_re = q_re * b_re - q_im * b_im
    bb_im = q_re * b_im + q_im * b_re
    c_re = cre_ref[...]
    c_im = cim_ref[...]
    g, h, p = c_re.shape

    pows = [(jnp.ones_like(l_re), jnp.zeros_like(l_re))]
    for _ in range(SSM_CHUNK):
        pr, pi = pows[-1]
        pows.append((pr * l_re - pi * l_im, pr * l_im + pi * l_re))
    lcre_ref[...] = pows[SSM_CHUNK][0]
    lcim_ref[...] = pows[SSM_CHUNK][1]

    for k in range(SSM_CHUNK):
        pr, pi = pows[SSM_CHUNK - 1 - k]
        pr, pi = pr[:, None, :], pi[:, None, :]
        ere_ref[k] = pr * bb_re - pi * bb_im
        eim_ref[k] = pr * bb_im + pi * bb_re
    for r in range(SSM_CHUNK):
        pr, pi = pows[r + 1]
        pr, pi = pr[:, None, :], pi[:, None, :]
        fre_ref[r] = c_re * pr - c_im * pi
        fim_ref[r] = -(c_re * pi + c_im * pr)
    nt = (((1,), (1,)), ((), ()))
    c_re2 = c_re.reshape(g * h, p)
    c_im2 = c_im.reshape(g * h, p)
    for j in range(SSM_CHUNK):
        pr, pi = pows[j]
        pr, pi = pr[:, None, :], pi[:, None, :]
        w_re = (pr * bb_re - pi * bb_im).reshape(g * h, p)
        w_im = (pr * bb_im + pi * bb_re).reshape(g * h, p)
        kk = (lax.dot_general(w_re, c_re2, nt, precision=lax.Precision.HIGHEST,
                              preferred_element_type=F32)
              - lax.dot_general(w_im, c_im2, nt, precision=lax.Precision.HIGHEST,
                                preferred_element_type=F32))
        for q in range(g * h // QUAD_CH):
            kq_ref[j, q] = kk[q * QUAD_CH:(q + 1) * QUAD_CH, q * QUAD_CH:(q + 1) * QUAD_CH]


def _ssm_prep(a_re, a_im, log_dt, b_re_t, b_im_t, c_re, c_im):
    g, p = a_re.shape
    h = b_re_t.shape[1]
    n = SSM_CHUNK
    return pl.pallas_call(
        _ssm_prep_kernel,
        out_shape=(jax.ShapeDtypeStruct((g, p), F32), jax.ShapeDtypeStruct((g, p), F32),
                   jax.ShapeDtypeStruct((n, g, h, p), F32), jax.ShapeDtypeStruct((n, g, h, p), F32),
                   jax.ShapeDtypeStruct((n, g, h, p), F32), jax.ShapeDtypeStruct((n, g, h, p), F32),
                   jax.ShapeDtypeStruct((n, g * h // QUAD_CH, QUAD_CH, QUAD_CH), F32)),
        name="ssm_prep",
    )(a_re, a_im, log_dt.reshape(g, 1), b_re_t, b_im_t, c_re, c_im)


def _mem_kv_kernel(mem_ref, g_ref, wkt_ref, wv_ref, kt_ref, v_ref):
    mn = _rms(mem_ref[...], g_ref[...]).astype(BF16)
    kt = lax.dot_general(wkt_ref[...], mn, (((1,), (1,)), ((), ())),
                         preferred_element_type=F32)
    kt_ref[...] = kt.astype(BF16)
    v_ref[...] = _dot(mn, wv_ref[...]).astype(BF16)


def _mem_kv(mem, g, wkt, wv):
    b, m, d = mem.shape
    w = wv.shape[1]
    return pl.pallas_call(
        _mem_kv_kernel,
        grid=(b,),
        in_specs=[pl.BlockSpec((None, m, d), lambda i: (i, 0, 0)),
                  pl.BlockSpec((1, d), lambda i: (0, 0)),
                  pl.BlockSpec((w, d), lambda i: (0, 0)),
                  pl.BlockSpec((d, w), lambda i: (0, 0))],
        out_specs=(pl.BlockSpec((None, w, m), lambda i: (i, 0, 0)),
                   pl.BlockSpec((None, m, w), lambda i: (i, 0, 0))),
        out_shape=(jax.ShapeDtypeStruct((b, w, m), BF16), jax.ShapeDtypeStruct((b, m, w), BF16)),
        name="mem_kv",
    )(mem, g, wkt, wv)


def _mixer_kernel(x_ref, g1_ref, win_ref, convw_ref, convp_ref, kt_ref, v_ref, attnp_ref,
                  u_ref, sgs_ref, part_ref, cv_ref, *, cw, sw, mw, d):
    i = pl.program_id(1)
    tb = x_ref.shape[0]
    h = _rms(x_ref[...], g1_ref[...]).astype(BF16)

    def proj(lo, hi):
        return _dot(h, win_ref[:, lo:hi])

    o_u = 3 * cw
    o_q = o_u + sw
    o_gc = o_q + mw
    o_gs = o_gc + d
    o_gm = o_gs + d

    q = proj(o_q, o_gc)
    scale = MEM_HEAD_DIM ** -0.5
    scores = []
    for hd in range(MEM_HEADS):
        lo = hd * MEM_HEAD_DIM
        scores.append(_dot(q[:, lo:lo + MEM_HEAD_DIM].astype(BF16),
                           kt_ref[lo:lo + MEM_HEAD_DIM, :]) * scale)
    c_gate = proj(cw, 2 * cw)
    v_conv = proj(2 * cw, 3 * cw)

    heads = []
    for hd in range(MEM_HEADS):
        lo = hd * MEM_HEAD_DIM
        s = scores[hd]
        e = jnp.exp(s - jnp.max(s, axis=-1, keepdims=True))
        pv = _dot(e.astype(BF16), v_ref[:, lo:lo + MEM_HEAD_DIM])
        heads.append(pv / jnp.sum(e, axis=-1, keepdims=True))
    o = jnp.concatenate(heads, axis=-1).astype(BF16)
    b_gate = proj(0, cw)

    cv = c_gate * v_conv

    @pl.when(i == 0)
    def _():
        cv_ref[0:SUBLANES, :] = jnp.zeros((SUBLANES, cw), F32)

    cv_ref[SUBLANES:SUBLANES + tb, :] = cv
    conv = (convw_ref[0:1, :] * cv_ref[SUBLANES - 2:SUBLANES - 2 + tb, :]
            + convw_ref[1:2, :] * cv_ref[SUBLANES - 1:SUBLANES - 1 + tb, :]
            + convw_ref[2:3, :] * cv)
    cv_ref[0:SUBLANES, :] = cv_ref[tb:tb + SUBLANES, :]
    pre = (b_gate * conv).astype(BF16)

    g_c = proj(o_gc, o_gs)
    y_mem = _dot(o, attnp_ref[...])
    y_conv = _dot(pre, convp_ref[...])
    g_m = proj(o_gm, o_gm + d)
    part_ref[...] = (jax.nn.sigmoid(g_c) * y_conv + jax.nn.sigmoid(g_m) * y_mem).astype(BF16)
    sgs_ref[...] = jax.nn.sigmoid(proj(o_gs, o_gm)).astype(BF16)
    u_ref[...] = proj(o_u, o_q).astype(BF16)


def _mixer(x, g1, w_in, conv_w, conv_p, kt, v, attn_p, *, cw, sw, mw):
    b, s, d = x.shape
    tb = MIXER_ROW_TILE
    m = kt.shape[2]
    kern = functools.partial(_mixer_kernel, cw=cw, sw=sw, mw=mw, d=d)
    return pl.pallas_call(
        kern,
        grid=(b, s // tb),
        in_specs=[pl.BlockSpec((None, tb, d), lambda bi, i: (bi, i, 0)),
                  _const_spec((1, d)),
                  _const_spec(w_in.shape),
                  _const_spec(conv_w.shape),
                  _const_spec(conv_p.shape),
                  pl.BlockSpec((None, mw, m), lambda bi, i: (bi, 0, 0)),
                  pl.BlockSpec((None, m, mw), lambda bi, i: (bi, 0, 0)),
                  _const_spec(attn_p.shape)],
        out_specs=(pl.BlockSpec((None, tb, sw), lambda bi, i: (bi, i, 0)),
                   pl.BlockSpec((None, tb, d), lambda bi, i: (bi, i, 0)),
                   pl.BlockSpec((None, tb, d), lambda bi, i: (bi, i, 0))),
        out_shape=(jax.ShapeDtypeStruct((b, s, sw), BF16),
                   jax.ShapeDtypeStruct((b, s, d), BF16),
                   jax.ShapeDtypeStruct((b, s, d), BF16)),
        scratch_shapes=[pltpu.VMEM((tb + SUBLANES, cw), F32)],
        compiler_params=pltpu.CompilerParams(
            dimension_semantics=("arbitrary", "arbitrary"),
            vmem_limit_bytes=VMEM_LIMIT_BYTES),
        name="mixer",
    )(x, g1, w_in, conv_w, conv_p, kt, v, attn_p)


def _s5_kernel(u_ref, ere_ref, eim_ref, lcre_ref, lcim_ref, fre_ref, fim_ref, tz_ref,
               gw_ref, gb_ref, sp_ref, y_ref,
               utm_ref, x_ref, st_ref, hst_ref, yc_ref, ytm_ref, *, n_state):
    batch, n_t, sw = u_ref.shape
    n_c = n_t // SSM_CHUNK
    n_q = sw // QUAD_CH
    tile = SSM_CHUNK * QUAD_CH
    half = LANES // 2
    lo = lax.broadcasted_iota(jnp.int32, (batch, LANES), 1) < half
    step_i = pl.program_id(0)
    slot_a = step_i % 2
    slot_b = 1 - slot_a

    @pl.when(step_i == 0)
    def _():
        hst_ref[...] = jnp.zeros_like(hst_ref)
        ytm_ref[1] = jnp.zeros(ytm_ref.shape[1:], F32)

    sub_t = POST_ROWS // batch

    def post(sb):
        y = ytm_ref[slot_b, sb * POST_ROWS:(sb + 1) * POST_ROWS, :].reshape(sub_t, batch, sw)
        y = jnp.swapaxes(y, 0, 1).reshape(POST_ROWS, sw)
        y = jax.nn.gelu(y)
        y = y * jax.nn.sigmoid(_dot(y.astype(BF16), gw_ref[...]) + gb_ref[...])
        out = _dot(y.astype(BF16), sp_ref[...]).astype(BF16)
        y_ref[:, sb * sub_t:(sb + 1) * sub_t, :] = out.reshape(batch, sub_t, -1)

    assert n_t // sub_t == 4, "stage B is interleaved as four sub-blocks"

    utm_ref[...] = jnp.swapaxes(u_ref[...].astype(F32), 0, 1).reshape(n_t * batch, sw)
    post(0)
    for c in range(n_c):
        r0 = c * batch
        for lc in range(sw // LANES):
            for kp in range(SSM_CHUNK // 2):
                t0 = (c * SSM_CHUNK + 2 * kp) * batch
                s0 = utm_ref[t0:t0 + batch, lc * LANES:(lc + 1) * LANES]
                s1 = utm_ref[t0 + batch:t0 + 2 * batch, lc * LANES:(lc + 1) * LANES]
                col = (2 * lc) * tile + kp * LANES
                x_ref[r0:r0 + batch, col:col + LANES] = jnp.where(lo, s0, pltpu.roll(s1, half, 1))
                col += tile
                x_ref[r0:r0 + batch, col:col + LANES] = jnp.where(lo, pltpu.roll(s0, half, 1), s1)

    for q in range(n_q):
        xq = x_ref[:, q * tile:(q + 1) * tile].astype(BF16)
        st_ref[:, q * tile:(q + 1) * tile] = _dot(xq, ere_ref[q])
        st_ref[:, n_state + q * tile:n_state + (q + 1) * tile] = _dot(xq, eim_ref[q])
    post(1)

    for cc in range(n_state // SCAN_COLS):
        re = slice(cc * SCAN_COLS, (cc + 1) * SCAN_COLS)
        im = slice(n_state + cc * SCAN_COLS, n_state + (cc + 1) * SCAN_COLS)
        l_re = lcre_ref[:, re]
        l_im = lcim_ref[:, re]
        h_re = hst_ref[:, re]
        h_im = hst_ref[:, im]
        for c in range(n_c):
            rows = slice(c * batch, (c + 1) * batch)
            b_re = st_ref[rows, re]
            b_im = st_ref[rows, im]
            st_ref[rows, re] = h_re
            st_ref[rows, im] = h_im
            h_re, h_im = (l_re * h_re - l_im * h_im + b_re, l_re * h_im + l_im * h_re + b_im)
        hst_ref[:, re] = h_re
        hst_ref[:, im] = h_im
    post(2)

    for q in range(n_q):
        cols = slice(q * tile, (q + 1) * tile)
        xq = x_ref[:, cols].astype(BF16)
        h_re = st_ref[:, cols].astype(BF16)
        h_im = st_ref[:, n_state + q * tile:n_state + (q + 1) * tile].astype(BF16)
        yc_ref[:, cols] = (_dot(h_re, fre_ref[q]) + _dot(h_im, fim_ref[q]) + _dot(xq, tz_ref[q]))
    post(3)

    for c in range(n_c):
        r0 = c * batch
        for j in range(sw // LANES):
            for rp in range(SSM_CHUNK // 2):
                col = (2 * j) * tile + rp * LANES
                a = yc_ref[r0:r0 + batch, col:col + LANES]
                b = yc_ref[r0:r0 + batch, col + tile:col + tile + LANES]
                t0 = (c * SSM_CHUNK + 2 * rp) * batch
                ytm_ref[slot_a, t0:t0 + batch, j * LANES:(j + 1) * LANES] = jnp.where(
                    lo, a, pltpu.roll(b, half, 1))
                ytm_ref[slot_a, t0 + batch:t0 + 2 * batch, j * LANES:(j + 1) * LANES] = jnp.where(
                    lo, pltpu.roll(a, half, 1), b)


def _s5(u, ere, eim, lcre, lcim, fre, fim, tz, glu_w, glu_b, ssm_p):
    batch, s, sw = u.shape
    d = ssm_p.shape[1]
    n_state = lcre.shape[1]
    n_t = TIME_TILE
    n_blocks = s // n_t
    rows_c = (n_t // SSM_CHUNK) * batch
    kern = functools.partial(_s5_kernel, n_state=n_state)
    consts = (ere, eim, lcre, lcim, fre, fim, tz, glu_w, glu_b, ssm_p)
    return pl.pallas_call(
        kern,
        grid=(n_blocks + 1,),
        in_specs=[pl.BlockSpec((batch, n_t, sw), lambda i: (0, jnp.minimum(i, n_blocks - 1), 0))]
                 + [_const_spec(c.shape) for c in consts],
        out_specs=pl.BlockSpec((batch, n_t, d), lambda i: (0, jnp.maximum(i - 1, 0), 0)),
        out_shape=jax.ShapeDtypeStruct((batch, s, d), BF16),
        scratch_shapes=[pltpu.VMEM((batch * n_t, sw), F32),
                        pltpu.VMEM((rows_c, SSM_CHUNK * sw), F32),
                        pltpu.VMEM((rows_c, 2 * n_state), F32),
                        pltpu.VMEM((batch, 2 * n_state), F32),
                        pltpu.VMEM((rows_c, SSM_CHUNK * sw), F32),
                        pltpu.VMEM((2, batch * n_t, sw), F32)],
        compiler_params=pltpu.CompilerParams(
            dimension_semantics=("arbitrary",),
            vmem_limit_bytes=VMEM_LIMIT_BYTES),
        name="s5",
    )(u, *consts)


def _ffn_kernel(x_ref, part_ref, sgs_ref, yssm_ref, wo_ref, g2_ref, wg_ref, wu_ref, wd_ref,
                gf_ref, o_ref, *, chunks):
    tb = x_ref.shape[0] // ROW_SUBTILES
    subs = [slice(sub * tb, (sub + 1) * tb) for sub in range(ROW_SUBTILES)]
    x1s = []
    for rows in subs:
        merged = (part_ref[rows, :].astype(F32)
                  + sgs_ref[rows, :].astype(F32) * yssm_ref[rows, :].astype(F32))
        x1s.append(x_ref[rows, :] + _dot(merged.astype(BF16), wo_ref[...]))
    h2s = [_rms(x1, g2_ref[...]).astype(BF16) for x1 in x1s]
    outs = []
    for x1, h2 in zip(x1s, h2s):
        acc = x1
        for lo, hi in chunks:
            g = _dot(h2, wg_ref[:, lo:hi])
            a = (g * jax.nn.sigmoid(g) * _dot(h2, wu_ref[:, lo:hi])).astype(BF16)
            acc = acc + _dot(a, wd_ref[lo:hi, :])
        outs.append(_rms(acc, gf_ref[...]))
    o_ref[...] = jnp.concatenate(outs, axis=0)


def _ffn_chunks(hidden):
    step = 4 * MXU_TILE
    return tuple((lo, min(lo + step, hidden)) for lo in range(0, hidden, step))


def _ffn(x, part, sgs, yssm, w_o, g2, wg, wu, wd, gf):
    b, s, d = x.shape
    tb = ROW_TILE
    kern = functools.partial(_ffn_kernel, chunks=_ffn_chunks(wg.shape[1]))
    row_spec = pl.BlockSpec((None, tb, d), lambda bi, i: (bi, i, 0))
    return pl.pallas_call(
        kern,
        grid=(b, s // tb),
        in_specs=[row_spec, row_spec, row_spec, row_spec,
                  _const_spec(w_o.shape), _const_spec((1, d)),
                  _const_spec(wg.shape), _const_spec(wu.shape), _const_spec(wd.shape),
                  _const_spec((1, d))],
        out_specs=row_spec,
        out_shape=jax.ShapeDtypeStruct((b, s, d), x.dtype),
        compiler_params=pltpu.CompilerParams(
            dimension_semantics=("arbitrary", "arbitrary"),
            vmem_limit_bytes=FFN_VMEM_LIMIT_BYTES),
        name="ffn",
    )(x, part, sgs, yssm, w_o, g2, wg, wu, wd, gf)


def _quad_eye(n):
    return jnp.eye(n, dtype=F32)


def _pack_e(e):
    k, g, h, p = e.shape
    a = QUAD_CH // h
    e5 = e.reshape(k, g // a, a, h, p)
    t = e5[:, :, :, :, None, :] * _quad_eye(a)[None, None, :, None, :, None]
    return t.transpose(1, 0, 2, 3, 4, 5).reshape(g // a, k * a * h, a * p)


def _pack_f(f):
    return _pack_e(f).transpose(0, 2, 1)


def _pack_toeplitz(kq, d_skip, h):
    n, n_q, qc, _ = kq.shape
    idx = jnp.arange(qc)
    same_group = (idx[:, None] // h == idx[None, :] // h).astype(F32)
    blk = kq * same_group
    blk = blk.at[0].add(jnp.eye(qc, dtype=F32) * d_skip.reshape(n_q, 1, qc))
    zero = jnp.zeros_like(blk[0])
    rows = [jnp.concatenate([blk[r - k] if r >= k else zero for r in range(n)], axis=-1)
            for k in range(n)]
    return jnp.concatenate(rows, axis=1)


def kernel(x, mem, norm1_g, w_in, conv_w, conv_proj, ssm_A_re, ssm_A_im, ssm_log_dt, ssm_B_re,
           ssm_B_im, ssm_C_re, ssm_C_im, ssm_D, ssm_glu_w, ssm_glu_b, ssm_proj, mem_norm_g,
           attn_wk, attn_wv, attn_proj, w_o, norm2_g, ffn_w_gate, ffn_w_up, ffn_w_down,
           final_norm_g):
    b, s, d = x.shape
    depth = w_in.shape[0]
    cw = conv_w.shape[2]
    sw = ssm_D.shape[1]
    mw = attn_wk.shape[2]
    n_groups = ssm_A_re.shape[1]
    n_state = n_groups * SSM_STATE
    assert depth == 1, "the final norm is fused into the single layer's ffn kernel"
    assert b == SUBLANES, "the S5 recurrence keeps the batch on the sublane axis"

    for l in range(depth):
        lc_re, lc_im, e_re, e_im, f_re, f_im, kq = _ssm_prep(
            ssm_A_re[l], ssm_A_im[l], ssm_log_dt[l],
            ssm_B_re[l].transpose(0, 2, 1), ssm_B_im[l].transpose(0, 2, 1),
            ssm_C_re[l], ssm_C_im[l])
        ere = _pack_e(e_re).astype(BF16)
        eim = _pack_e(e_im).astype(BF16)
        fre = _pack_f(f_re).astype(BF16)
        fim = _pack_f(f_im).astype(BF16)
        tz = _pack_toeplitz(kq, ssm_D[l], SSM_GROUP).astype(BF16)
        lcre = jnp.broadcast_to(lc_re.reshape(1, n_state), (b, n_state))
        lcim = jnp.broadcast_to(lc_im.reshape(1, n_state), (b, n_state))

        kt, v = _mem_kv(mem, mem_norm_g[l].reshape(1, d), attn_wk[l].T.astype(BF16),
                        attn_wv[l].astype(BF16))
        u, sgs, part = _mixer(
            x, norm1_g[l].reshape(1, d), w_in[l].astype(BF16), conv_w[l],
            conv_proj[l].astype(BF16), kt, v, attn_proj[l].astype(BF16), cw=cw, sw=sw, mw=mw)
        yssm = _s5(u, ere, eim, lcre, lcim, fre, fim, tz,
                   ssm_glu_w[l].astype(BF16), ssm_glu_b[l].reshape(1, sw),
                   ssm_proj[l].astype(BF16))
        x = _ffn(x, part, sgs, yssm, w_o[l].astype(BF16),
                 norm2_g[l].reshape(1, d), ffn_w_gate[l].astype(BF16), ffn_w_up[l].astype(BF16),
                 ffn_w_down[l].astype(BF16), final_norm_g.reshape(1, d))
    return x
```

```python
import functools
import math

import jax
import jax.numpy as jnp
from jax import lax
from jax.experimental import pallas as pl
from jax.experimental.pallas import tpu as pltpu

F32 = jnp.float32
BF16 = jnp.bfloat16

EPS = 1e-6
CONV_K = 3
SSM_GROUP = 16
SSM_STATE = 64
MEM_HEADS = 4
MEM_HEAD_DIM = 128

MXU_TILE = 256
SUBLANES = 8
LANES = 128
VMEM_LIMIT_BYTES = 56 * 1024 * 1024
FFN_VMEM_LIMIT_BYTES = 62 * 1024 * 1024

MIXER_ROW_TILE = 1024
ROW_TILE = 1024
ROW_SUBTILES = 4
TIME_TILE = 256
SSM_CHUNK = 4
QUAD_CH = MXU_TILE // SSM_CHUNK
SCAN_COLS = 512
POST_ROWS = 512


def _rms(x, g):
    ms = jnp.mean(x * x, axis=-1, keepdims=True)
    return x * lax.rsqrt(ms + EPS) * g


def _dot(a, b):
    return jnp.dot(a, b, preferred_element_type=F32)


def _const_spec(shape):
    nd = len(shape)
    return pl.BlockSpec(shape, lambda *_: (0,) * nd, pipeline_mode=pl.Buffered(1))


def _ssm_prep_kernel(are_ref, aim_ref, ldt_ref, bre_ref, bim_ref, cre_ref, cim_ref,
                     lcre_ref, lcim_ref, ere_ref, eim_ref, fre_ref, fim_ref, kq_ref):
    a_re = are_ref[...]
    a_im = aim_ref[...]
    dt = jnp.exp(ldt_ref[...])
    mag = jnp.exp(a_re * dt)
    l_re = mag * jnp.cos(a_im * dt)
    l_im = mag * jnp.sin(a_im * dt)
    n_re = l_re - 1.0
    n_im = l_im
    den = a_re * a_re + a_im * a_im
    q_re = ((n_re * a_re + n_im * a_im) / den)[:, None, :]
    q_im = ((n_im * a_re - n_re * a_im) / den)[:, None, :]
    b_re = bre_ref[...]
    b_im = bim_ref[...]
    bb_re = q_re * b_re - q_im * b_im
    bb_im = q_re * b_im + q_im * b_re
    c_re = cre_ref[...]
    c_im = cim_ref[...]
    g, h, p = c_re.shape

    pows = [(jnp.ones_like(l_re), jnp.zeros_like(l_re))]
    for _ in range(SSM_CHUNK):
        pr, pi = pows[-1]
        pows.append((pr * l_re - pi * l_im, pr * l_im + pi * l_re))
    lcre_ref[...] = pows[SSM_CHUNK][0]
    lcim_ref[...] = pows[SSM_CHUNK][1]

    def place(out_ref, blocks):
        per_quad = QUAD_CH // h
        out_ref[...] = jnp.zeros(out_ref.shape, F32)
        for k, blk in enumerate(blocks):
            for gi in range(g):
                a = gi % per_quad
                r0 = (k * per_quad + a) * h
                out_ref[gi // per_quad, r0:r0 + h, a * p:(a + 1) * p] = blk[gi]

    e_re, e_im, f_re, f_im = [], [], [], []
    for k in range(SSM_CHUNK):
        pr, pi = pows[SSM_CHUNK - 1 - k]
        pr, pi = pr[:, None, :], pi[:, None, :]
        e_re.append(pr * bb_re - pi * bb_im)
        e_im.append(pr * bb_im + pi * bb_re)
    for r in range(SSM_CHUNK):
        pr, pi = pows[r + 1]
        pr, pi = pr[:, None, :], pi[:, None, :]
        f_re.append(c_re * pr - c_im * pi)
        f_im.append(-(c_re * pi + c_im * pr))
    place(ere_ref, e_re)
    place(eim_ref, e_im)
    place(fre_ref, f_re)
    place(fim_ref, f_im)
    nt = (((1,), (1,)), ((), ()))
    c_re2 = c_re.reshape(g * h, p)
    c_im2 = c_im.reshape(g * h, p)
    for j in range(SSM_CHUNK):
        pr, pi = pows[j]
        pr, pi = pr[:, None, :], pi[:, None, :]
        w_re = (pr * bb_re - pi * bb_im).reshape(g * h, p)
        w_im = (pr * bb_im + pi * bb_re).reshape(g * h, p)
        kk = (lax.dot_general(w_re, c_re2, nt, precision=lax.Precision.HIGHEST,
                              preferred_element_type=F32)
              - lax.dot_general(w_im, c_im2, nt, precision=lax.Precision.HIGHEST,
                                preferred_element_type=F32))
        for q in range(g * h // QUAD_CH):
            kq_ref[j, q] = kk[q * QUAD_CH:(q + 1) * QUAD_CH, q * QUAD_CH:(q + 1) * QUAD_CH]


def _ssm_prep(a_re, a_im, log_dt, b_re_t, b_im_t, c_re, c_im):
    g, p = a_re.shape
    h = b_re_t.shape[1]
    n = SSM_CHUNK
    return pl.pallas_call(
        _ssm_prep_kernel,
        out_shape=(jax.ShapeDtypeStruct((g, p), F32), jax.ShapeDtypeStruct((g, p), F32),
                   *[jax.ShapeDtypeStruct((g * h // QUAD_CH, n * QUAD_CH, n * QUAD_CH), F32)] * 4,
                   jax.ShapeDtypeStruct((n, g * h // QUAD_CH, QUAD_CH, QUAD_CH), F32)),
        name="ssm_prep",
    )(a_re, a_im, log_dt.reshape(g, 1), b_re_t, b_im_t, c_re, c_im)


def _mem_kv_kernel(mem_ref, g_ref, wkt_ref, wv_ref, kt_ref, v_ref):
    mn = _rms(mem_ref[...], g_ref[...]).astype(BF16)
    kt = lax.dot_general(wkt_ref[...], mn, (((1,), (1,)), ((), ())),
                         preferred_element_type=F32)
    kt_ref[...] = kt.astype(BF16)
    v_ref[...] = _dot(mn, wv_ref[...]).astype(BF16)


def _mem_kv(mem, g, wkt, wv):
    b, m, d = mem.shape
    w = wv.shape[1]
    return pl.pallas_call(
        _mem_kv_kernel,
        grid=(b,),
        in_specs=[pl.BlockSpec((None, m, d), lambda i: (i, 0, 0)),
                  pl.BlockSpec((1, d), lambda i: (0, 0)),
                  pl.BlockSpec((w, d), lambda i: (0, 0)),
                  pl.BlockSpec((d, w), lambda i: (0, 0))],
        out_specs=(pl.BlockSpec((None, w, m), lambda i: (i, 0, 0)),
                   pl.BlockSpec((None, m, w), lambda i: (i, 0, 0))),
        out_shape=(jax.ShapeDtypeStruct((b, w, m), BF16), jax.ShapeDtypeStruct((b, m, w), BF16)),
        name="mem_kv",
    )(mem, g, wkt, wv)


def _mixer_kernel(x_ref, g1_ref, win_ref, convw_ref, convp_ref, kt_ref, v_ref, attnp_ref,
                  u_ref, sgs_ref, part_ref, cv_ref, *, cw, sw, mw, d):
    i = pl.program_id(1)
    tb = x_ref.shape[0]
    h = _rms(x_ref[...], g1_ref[...]).astype(BF16)

    def proj(lo, hi):
        return _dot(h, win_ref[:, lo:hi])

    o_u = 3 * cw
    o_q = o_u + sw
    o_gc = o_q + mw
    o_gs = o_gc + d
    o_gm = o_gs + d

    q = proj(o_q, o_gc)
    scale = MEM_HEAD_DIM ** -0.5
    scores = []
    for hd in range(MEM_HEADS):
        lo = hd * MEM_HEAD_DIM
        scores.append(_dot(q[:, lo:lo + MEM_HEAD_DIM].astype(BF16),
                           kt_ref[lo:lo + MEM_HEAD_DIM, :]) * scale)
    c_gate = proj(cw, 2 * cw)
    v_conv = proj(2 * cw, 3 * cw)

    heads = []
    for hd in range(MEM_HEADS):
        lo = hd * MEM_HEAD_DIM
        s = scores[hd]
        e = jnp.exp(s - jnp.max(s, axis=-1, keepdims=True))
        pv = _dot(e.astype(BF16), v_ref[:, lo:lo + MEM_HEAD_DIM])
        heads.append(pv / jnp.sum(e, axis=-1, keepdims=True))
    o = jnp.concatenate(heads, axis=-1).astype(BF16)
    b_gate = proj(0, cw)

    cv = c_gate * v_conv

    @pl.when(i == 0)
    def _():
        cv_ref[0:SUBLANES, :] = jnp.zeros((SUBLANES, cw), F32)

    cv_ref[SUBLANES:SUBLANES + tb, :] = cv
    conv = (convw_ref[0:1, :] * cv_ref[SUBLANES - 2:SUBLANES - 2 + tb, :]
            + convw_ref[1:2, :] * cv_ref[SUBLANES - 1:SUBLANES - 1 + tb, :]
            + convw_ref[2:3, :] * cv)
    cv_ref[0:SUBLANES, :] = cv_ref[tb:tb + SUBLANES, :]
    pre = (b_gate * conv).astype(BF16)

    g_c = proj(o_gc, o_gs)
    y_mem = _dot(o, attnp_ref[...])
    y_conv = _dot(pre, convp_ref[...])
    g_m = proj(o_gm, o_gm + d)
    part_ref[...] = (jax.nn.sigmoid(g_c) * y_conv + jax.nn.sigmoid(g_m) * y_mem).astype(BF16)
    sgs_ref[...] = jax.nn.sigmoid(proj(o_gs, o_gm)).astype(BF16)
    u_ref[...] = proj(o_u, o_q).astype(BF16)


def _mixer(x, g1, w_in, conv_w, conv_p, kt, v, attn_p, *, cw, sw, mw):
    b, s, d = x.shape
    tb = MIXER_ROW_TILE
    m = kt.shape[2]
    kern = functools.partial(_mixer_kernel, cw=cw, sw=sw, mw=mw, d=d)
    return pl.pallas_call(
        kern,
        grid=(b, s // tb),
        in_specs=[pl.BlockSpec((None, tb, d), lambda bi, i: (bi, i, 0)),
                  _const_spec((1, d)),
                  _const_spec(w_in.shape),
                  _const_spec(conv_w.shape),
                  _const_spec(conv_p.shape),
                  pl.BlockSpec((None, mw, m), lambda bi, i: (bi, 0, 0)),
                  pl.BlockSpec((None, m, mw), lambda bi, i: (bi, 0, 0)),
                  _const_spec(attn_p.shape)],
        out_specs=(pl.BlockSpec((None, tb, sw), lambda bi, i: (bi, i, 0)),
                   pl.BlockSpec((None, tb, d), lambda bi, i: (bi, i, 0)),
                   pl.BlockSpec((None, tb, d), lambda bi, i: (bi, i, 0))),
        out_shape=(jax.ShapeDtypeStruct((b, s, sw), BF16),
                   jax.ShapeDtypeStruct((b, s, d), BF16),
                   jax.ShapeDtypeStruct((b, s, d), BF16)),
        scratch_shapes=[pltpu.VMEM((tb + SUBLANES, cw), F32)],
        compiler_params=pltpu.CompilerParams(
            dimension_semantics=("arbitrary", "arbitrary"),
            vmem_limit_bytes=VMEM_LIMIT_BYTES),
        name="mixer",
    )(x, g1, w_in, conv_w, conv_p, kt, v, attn_p)


def _s5_kernel(u_ref, ere_ref, eim_ref, lcre_ref, lcim_ref, fre_ref, fim_ref, tz_ref,
               gw_ref, gb_ref, sp_ref, y_ref,
               utm_ref, x_ref, st_ref, hst_ref, yc_ref, ytm_ref, *, n_state):
    batch, n_t, sw = u_ref.shape
    n_c = n_t // SSM_CHUNK
    n_q = sw // QUAD_CH
    tile = SSM_CHUNK * QUAD_CH
    half = LANES // 2
    lo = lax.broadcasted_iota(jnp.int32, (batch, LANES), 1) < half
    step_i = pl.program_id(0)
    slot_a = step_i % 2
    slot_b = 1 - slot_a

    @pl.when(step_i == 0)
    def _():
        hst_ref[...] = jnp.zeros_like(hst_ref)
        ytm_ref[1] = jnp.zeros(ytm_ref.shape[1:], F32)

    sub_t = POST_ROWS // batch

    def post(sb):
        y = ytm_ref[slot_b, sb * POST_ROWS:(sb + 1) * POST_ROWS, :].reshape(sub_t, batch, sw)
        y = jnp.swapaxes(y, 0, 1).reshape(POST_ROWS, sw)
        y = jax.nn.gelu(y)
        y = y * jax.nn.sigmoid(_dot(y.astype(BF16), gw_ref[...]) + gb_ref[...])
        out = _dot(y.astype(BF16), sp_ref[...]).astype(BF16)
        y_ref[:, sb * sub_t:(sb + 1) * sub_t, :] = out.reshape(batch, sub_t, -1)

    assert n_t // sub_t == 4, "stage B is interleaved as four sub-blocks"

    utm_ref[...] = jnp.swapaxes(u_ref[...].astype(F32), 0, 1).reshape(n_t * batch, sw)
    post(0)
    for c in range(n_c):
        r0 = c * batch
        for lc in range(sw // LANES):
            for kp in range(SSM_CHUNK // 2):
                t0 = (c * SSM_CHUNK + 2 * kp) * batch
                s0 = utm_ref[t0:t0 + batch, lc * LANES:(lc + 1) * LANES]
                s1 = utm_ref[t0 + batch:t0 + 2 * batch, lc * LANES:(lc + 1) * LANES]
                col = (2 * lc) * tile + kp * LANES
                x_ref[r0:r0 + batch, col:col + LANES] = jnp.where(lo, s0, pltpu.roll(s1, half, 1))
                col += tile
                x_ref[r0:r0 + batch, col:col + LANES] = jnp.where(lo, pltpu.roll(s0, half, 1), s1)

    for q in range(n_q):
        xq = x_ref[:, q * tile:(q + 1) * tile].astype(BF16)
        st_ref[:, q * tile:(q + 1) * tile] = _dot(xq, ere_ref[q])
        st_ref[:, n_state + q * tile:n_state + (q + 1) * tile] = _dot(xq, eim_ref[q])
    post(1)

    for cc in range(n_state // SCAN_COLS):
        re = slice(cc * SCAN_COLS, (cc + 1) * SCAN_COLS)
        im = slice(n_state + cc * SCAN_COLS, n_state + (cc + 1) * SCAN_COLS)
        l_re = lcre_ref[:, re]
        l_im = lcim_ref[:, re]
        h_re = hst_ref[:, re]
        h_im = hst_ref[:, im]
        for c in range(n_c):
            rows = slice(c * batch, (c + 1) * batch)
            b_re = st_ref[rows, re]
            b_im = st_ref[rows, im]
            st_ref[rows, re] = h_re
            st_ref[rows, im] = h_im
            h_re, h_im = (l_re * h_re - l_im * h_im + b_re, l_re * h_im + l_im * h_re + b_im)
        hst_ref[:, re] = h_re
        hst_ref[:, im] = h_im
    post(2)

    for q in range(n_q):
        cols = slice(q * tile, (q + 1) * tile)
        xq = x_ref[:, cols].astype(BF16)
        h_re = st_ref[:, cols].astype(BF16)
        h_im = st_ref[:, n_state + q * tile:n_state + (q + 1) * tile].astype(BF16)
        yc_ref[:, cols] = (_dot(h_re, fre_ref[q]) + _dot(h_im, fim_ref[q]) + _dot(xq, tz_ref[q]))
    post(3)

    for c in range(n_c):
        r0 = c * batch
        for j in range(sw // LANES):
            for rp in range(SSM_CHUNK // 2):
                col = (2 * j) * tile + rp * LANES
                a = yc_ref[r0:r0 + batch, col:col + LANES]
                b = yc_ref[r0:r0 + batch, col + tile:col + tile + LANES]
                t0 = (c * SSM_CHUNK + 2 * rp) * batch
                ytm_ref[slot_a, t0:t0 + batch, j * LANES:(j + 1) * LANES] = jnp.where(
                    lo, a, pltpu.roll(b, half, 1))
                ytm_ref[slot_a, t0 + batch:t0 + 2 * batch, j * LANES:(j + 1) * LANES] = jnp.where(
                    lo, pltpu.roll(a, half, 1), b)


def _s5(u, ere, eim, lcre, lcim, fre, fim, tz, glu_w, glu_b, ssm_p):
    batch, s, sw = u.shape
    d = ssm_p.shape[1]
    n_state = lcre.shape[1]
    n_t = TIME_TILE
    n_blocks = s // n_t
    rows_c = (n_t // SSM_CHUNK) * batch
    kern = functools.partial(_s5_kernel, n_state=n_state)
    consts = (ere, eim, lcre, lcim, fre, fim, tz, glu_w, glu_b, ssm_p)
    return pl.pallas_call(
        kern,
        grid=(n_blocks + 1,),
        in_specs=[pl.BlockSpec((batch, n_t, sw), lambda i: (0, jnp.minimum(i, n_blocks - 1), 0))]
                 + [_const_spec(c.shape) for c in consts],
        out_specs=pl.BlockSpec((batch, n_t, d), lambda i: (0, jnp.maximum(i - 1, 0), 0)),
        out_shape=jax.ShapeDtypeStruct((batch, s, d), BF16),
        scratch_shapes=[pltpu.VMEM((batch * n_t, sw), F32),
                        pltpu.VMEM((rows_c, SSM_CHUNK * sw), F32),
                        pltpu.VMEM((rows_c, 2 * n_state), F32),
                        pltpu.VMEM((batch, 2 * n_state), F32),
                        pltpu.VMEM((rows_c, SSM_CHUNK * sw), F32),
                        pltpu.VMEM((2, batch * n_t, sw), F32)],
        compiler_params=pltpu.CompilerParams(
            dimension_semantics=("arbitrary",),
            vmem_limit_bytes=VMEM_LIMIT_BYTES),
        name="s5",
    )(u, *consts)


def _ffn_kernel(x_ref, part_ref, sgs_ref, yssm_ref, wo_ref, g2_ref, wg_ref, wu_ref, wd_ref,
                gf_ref, o_ref, *, chunks):
    tb = x_ref.shape[0] // ROW_SUBTILES
    subs = [slice(sub * tb, (sub + 1) * tb) for sub in range(ROW_SUBTILES)]
    x1s = []
    for rows in subs:
        merged = (part_ref[rows, :].astype(F32)
                  + sgs_ref[rows, :].astype(F32) * yssm_ref[rows, :].astype(F32))
        x1s.append(x_ref[rows, :] + _dot(merged.astype(BF16), wo_ref[...]))
    h2s = [_rms(x1, g2_ref[...]).astype(BF16) for x1 in x1s]
    outs = []
    for x1, h2 in zip(x1s, h2s):
        acc = x1
        for lo, hi in chunks:
            g = _dot(h2, wg_ref[:, lo:hi])
            a = (g * jax.nn.sigmoid(g) * _dot(h2, wu_ref[:, lo:hi])).astype(BF16)
            acc = acc + _dot(a, wd_ref[lo:hi, :])
        outs.append(_rms(acc, gf_ref[...]))
    o_ref[...] = jnp.concatenate(outs, axis=0)


def _ffn_chunks(hidden):
    step = 4 * MXU_TILE
    return tuple((lo, min(lo + step, hidden)) for lo in range(0, hidden, step))


def _ffn(x, part, sgs, yssm, w_o, g2, wg, wu, wd, gf):
    b, s, d = x.shape
    tb = ROW_TILE
    kern = functools.partial(_ffn_kernel, chunks=_ffn_chunks(wg.shape[1]))
    row_spec = pl.BlockSpec((None, tb, d), lambda bi, i: (bi, i, 0))
    return pl.pallas_call(
        kern,
        grid=(b, s // tb),
        in_specs=[row_spec, row_spec, row_spec, row_spec,
                  _const_spec(w_o.shape), _const_spec((1, d)),
                  _const_spec(wg.shape), _const_spec(wu.shape), _const_spec(wd.shape),
                  _const_spec((1, d))],
        out_specs=row_spec,
        out_shape=jax.ShapeDtypeStruct((b, s, d), x.dtype),
        compiler_params=pltpu.CompilerParams(
            dimension_semantics=("arbitrary", "arbitrary"),
            vmem_limit_bytes=FFN_VMEM_LIMIT_BYTES),
        name="ffn",
    )(x, part, sgs, yssm, w_o, g2, wg, wu, wd, gf)


def _pack_toeplitz(kq, d_skip, h):
    n, n_q, qc, _ = kq.shape
    idx = jnp.arange(qc)
    same_group = (idx[:, None] // h == idx[None, :] // h).astype(F32)
    blk = kq * same_group
    blk = blk.at[0].add(jnp.eye(qc, dtype=F32) * d_skip.reshape(n_q, 1, qc))
    zero = jnp.zeros_like(blk[0])
    rows = [jnp.concatenate([blk[r - k] if r >= k else zero for r in range(n)], axis=-1)
            for k in range(n)]
    return jnp.concatenate(rows, axis=1)


def kernel(x, mem, norm1_g, w_in, conv_w, conv_proj, ssm_A_re, ssm_A_im, ssm_log_dt, ssm_B_re,
           ssm_B_im, ssm_C_re, ssm_C_im, ssm_D, ssm_glu_w, ssm_glu_b, ssm_proj, mem_norm_g,
           attn_wk, attn_wv, attn_proj, w_o, norm2_g, ffn_w_gate, ffn_w_up, ffn_w_down,
           final_norm_g):
    b, s, d = x.shape
    depth = w_in.shape[0]
    cw = conv_w.shape[2]
    sw = ssm_D.shape[1]
    mw = attn_wk.shape[2]
    n_groups = ssm_A_re.shape[1]
    n_state = n_groups * SSM_STATE
    assert depth == 1, "the final norm is fused into the single layer's ffn kernel"
    assert b == SUBLANES, "the S5 recurrence keeps the batch on the sublane axis"

    for l in range(depth):
        lc_re, lc_im, e_re, e_im, f_re, f_im, kq = _ssm_prep(
            ssm_A_re[l], ssm_A_im[l], ssm_log_dt[l],
            ssm_B_re[l].transpose(0, 2, 1), ssm_B_im[l].transpose(0, 2, 1),
            ssm_C_re[l], ssm_C_im[l])
        ere = e_re.astype(BF16)
        eim = e_im.astype(BF16)
        fre = f_re.transpose(0, 2, 1).astype(BF16)
        fim = f_im.transpose(0, 2, 1).astype(BF16)
        tz = _pack_toeplitz(kq, ssm_D[l], SSM_GROUP).astype(BF16)
        lcre = jnp.broadcast_to(lc_re.reshape(1, n_state), (b, n_state))
        lcim = jnp.broadcast_to(lc_im.reshape(1, n_state), (b, n_state))

        kt, v = _mem_kv(mem, mem_norm_g[l].reshape(1, d), attn_wk[l].T.astype(BF16),
                        attn_wv[l].astype(BF16))
        u, sgs, part = _mixer(
            x, norm1_g[l].reshape(1, d), w_in[l].astype(BF16), conv_w[l],
            conv_proj[l].astype(BF16), kt, v, attn_proj[l].astype(BF16), cw=cw, sw=sw, mw=mw)
        yssm = _s5(u, ere, eim, lcre, lcim, fre, fim, tz,
                   ssm_glu_w[l].astype(BF16), ssm_glu_b[l].reshape(1, sw),
                   ssm_proj[l].astype(BF16))
        x = _ffn(x, part, sgs, yssm, w_o[l].astype(BF16),
                 norm2_g[l].reshape(1, d), ffn_w_gate[l].astype(BF16), ffn_w_up[l].astype(BF16),
                 ffn_w_down[l].astype(BF16), final_norm_g.reshape(1, d))
    return x
```

```python
import functools
import math

import jax
import jax.numpy as jnp
from jax import lax
from jax.experimental import pallas as pl
from jax.experimental.pallas import tpu as pltpu

F32 = jnp.float32
BF16 = jnp.bfloat16

EPS = 1e-6
CONV_K = 3
SSM_GROUP = 16
SSM_STATE = 64
MEM_HEADS = 4
MEM_HEAD_DIM = 128

MXU_TILE = 256
SUBLANES = 8
LANES = 128
VMEM_LIMIT_BYTES = 56 * 1024 * 1024
FFN_VMEM_LIMIT_BYTES = 62 * 1024 * 1024

MIXER_ROW_TILE = 1024
ROW_TILE = 1024
ROW_SUBTILES = 4
TIME_TILE = 256
SSM_CHUNK = 4
QUAD_CH = MXU_TILE // SSM_CHUNK
POST_ROWS = 512


def _rms(x, g):
    ms = jnp.mean(x * x, axis=-1, keepdims=True)
    return x * lax.rsqrt(ms + EPS) * g


def _dot(a, b):
    return jnp.dot(a, b, preferred_element_type=F32)


def _gelu_tanh(x):
    k0 = -2.0 * math.sqrt(2.0 / math.pi) * math.log2(math.e)
    w = x * (k0 + (k0 * 0.044715) * (x * x))
    return x / (1.0 + jnp.exp2(w))


def _const_spec(shape):
    nd = len(shape)
    return pl.BlockSpec(shape, lambda *_: (0,) * nd, pipeline_mode=pl.Buffered(1))


def _ssm_prep_kernel(are_ref, aim_ref, ldt_ref, bre_ref, bim_ref, cre_ref, cim_ref,
                     lcre_ref, lcim_ref, ere_ref, eim_ref, fre_ref, fim_ref, kq_ref):
    a_re = are_ref[...]
    a_im = aim_ref[...]
    dt = jnp.exp(ldt_ref[...])
    mag = jnp.exp(a_re * dt)
    l_re = mag * jnp.cos(a_im * dt)
    l_im = mag * jnp.sin(a_im * dt)
    n_re = l_re - 1.0
    n_im = l_im
    den = a_re * a_re + a_im * a_im
    q_re = ((n_re * a_re + n_im * a_im) / den)[:, None, :]
    q_im = ((n_im * a_re - n_re * a_im) / den)[:, None, :]
    b_re = bre_ref[...]
    b_im = bim_ref[...]
    bb_re = q_re * b_re - q_im * b_im
    bb_im = q_re * b_im + q_im * b_re
    c_re = cre_ref[...]
    c_im = cim_ref[...]
    g, h, p = c_re.shape

    pows = [(jnp.ones_like(l_re), jnp.zeros_like(l_re))]
    for _ in range(SSM_CHUNK):
        pr, pi = pows[-1]
        pows.append((pr * l_re - pi * l_im, pr * l_im + pi * l_re))
    lcre_ref[...] = pows[SSM_CHUNK][0]
    lcim_ref[...] = pows[SSM_CHUNK][1]

    def place(out_ref, blocks):
        per_quad = QUAD_CH // h
        out_ref[...] = jnp.zeros(out_ref.shape, F32)
        for k, blk in enumerate(blocks):
            for gi in range(g):
                a = gi % per_quad
                r0 = (k * per_quad + a) * h
                out_ref[gi // per_quad, r0:r0 + h, a * p:(a + 1) * p] = blk[gi]

    e_re, e_im, f_re, f_im = [], [], [], []
    for k in range(SSM_CHUNK):
        pr, pi = pows[SSM_CHUNK - 1 - k]
        pr, pi = pr[:, None, :], pi[:, None, :]
        e_re.append(pr * bb_re - pi * bb_im)
        e_im.append(pr * bb_im + pi * bb_re)
    for r in range(SSM_CHUNK):
        pr, pi = pows[r + 1]
        pr, pi = pr[:, None, :], pi[:, None, :]
        f_re.append(c_re * pr - c_im * pi)
        f_im.append(-(c_re * pi + c_im * pr))
    place(ere_ref, e_re)
    place(eim_ref, e_im)
    place(fre_ref, f_re)
    place(fim_ref, f_im)
    nt = (((1,), (1,)), ((), ()))
    c_re2 = c_re.reshape(g * h, p)
    c_im2 = c_im.reshape(g * h, p)
    for j in range(SSM_CHUNK):
        pr, pi = pows[j]
        pr, pi = pr[:, None, :], pi[:, None, :]
        w_re = (pr * bb_re - pi * bb_im).reshape(g * h, p)
        w_im = (pr * bb_im + pi * bb_re).reshape(g * h, p)
        kk = (lax.dot_general(w_re, c_re2, nt, precision=lax.Precision.HIGHEST,
                              preferred_element_type=F32)
              - lax.dot_general(w_im, c_im2, nt, precision=lax.Precision.HIGHEST,
                                preferred_element_type=F32))
        for q in range(g * h // QUAD_CH):
            kq_ref[j, q] = kk[q * QUAD_CH:(q + 1) * QUAD_CH, q * QUAD_CH:(q + 1) * QUAD_CH]


def _ssm_prep(a_re, a_im, log_dt, b_re_t, b_im_t, c_re, c_im):
    g, p = a_re.shape
    h = b_re_t.shape[1]
    n = SSM_CHUNK
    return pl.pallas_call(
        _ssm_prep_kernel,
        out_shape=(jax.ShapeDtypeStruct((g, p), F32), jax.ShapeDtypeStruct((g, p), F32),
                   *[jax.ShapeDtypeStruct((g * h // QUAD_CH, n * QUAD_CH, n * QUAD_CH), F32)] * 4,
                   jax.ShapeDtypeStruct((n, g * h // QUAD_CH, QUAD_CH, QUAD_CH), F32)),
        name="ssm_prep",
    )(a_re, a_im, log_dt.reshape(g, 1), b_re_t, b_im_t, c_re, c_im)


def _mem_kv_kernel(mem_ref, g_ref, wkt_ref, wv_ref, kt_ref, v_ref):
    mn = _rms(mem_ref[...], g_ref[...]).astype(BF16)
    kt = lax.dot_general(wkt_ref[...], mn, (((1,), (1,)), ((), ())),
                         preferred_element_type=F32)
    kt_ref[...] = kt.astype(BF16)
    v_ref[...] = _dot(mn, wv_ref[...]).astype(BF16)


def _mem_kv(mem, g, wkt, wv):
    b, m, d = mem.shape
    w = wv.shape[1]
    return pl.pallas_call(
        _mem_kv_kernel,
        grid=(b,),
        in_specs=[pl.BlockSpec((None, m, d), lambda i: (i, 0, 0)),
                  pl.BlockSpec((1, d), lambda i: (0, 0)),
                  pl.BlockSpec((w, d), lambda i: (0, 0)),
                  pl.BlockSpec((d, w), lambda i: (0, 0))],
        out_specs=(pl.BlockSpec((None, w, m), lambda i: (i, 0, 0)),
                   pl.BlockSpec((None, m, w), lambda i: (i, 0, 0))),
        out_shape=(jax.ShapeDtypeStruct((b, w, m), BF16), jax.ShapeDtypeStruct((b, m, w), BF16)),
        name="mem_kv",
    )(mem, g, wkt, wv)


def _mixer_kernel(x_ref, g1_ref, win_ref, convw_ref, convp_ref, kt_ref, v_ref, attnp_ref,
                  u_ref, sgs_ref, part_ref, cv_ref, *, cw, sw, mw, d):
    i = pl.program_id(1)
    tb = x_ref.shape[0]
    h = _rms(x_ref[...], g1_ref[...]).astype(BF16)

    def proj(lo, hi):
        return _dot(h, win_ref[:, lo:hi])

    o_u = 3 * cw
    o_q = o_u + sw
    o_gc = o_q + mw
    o_gs = o_gc + d
    o_gm = o_gs + d

    q = proj(o_q, o_gc)
    scale = MEM_HEAD_DIM ** -0.5
    scores = []
    for hd in range(MEM_HEADS):
        lo = hd * MEM_HEAD_DIM
        scores.append(_dot(q[:, lo:lo + MEM_HEAD_DIM].astype(BF16),
                           kt_ref[lo:lo + MEM_HEAD_DIM, :]) * scale)
    c_gate = proj(cw, 2 * cw)
    v_conv = proj(2 * cw, 3 * cw)

    heads = []
    for hd in range(MEM_HEADS):
        lo = hd * MEM_HEAD_DIM
        s = scores[hd]
        e = jnp.exp(s - jnp.max(s, axis=-1, keepdims=True))
        pv = _dot(e.astype(BF16), v_ref[:, lo:lo + MEM_HEAD_DIM])
        heads.append(pv / jnp.sum(e, axis=-1, keepdims=True))
    o = jnp.concatenate(heads, axis=-1).astype(BF16)
    b_gate = proj(0, cw)

    cv = c_gate * v_conv

    @pl.when(i == 0)
    def _():
        cv_ref[0:SUBLANES, :] = jnp.zeros((SUBLANES, cw), F32)

    cv_ref[SUBLANES:SUBLANES + tb, :] = cv
    conv = (convw_ref[0:1, :] * cv_ref[SUBLANES - 2:SUBLANES - 2 + tb, :]
            + convw_ref[1:2, :] * cv_ref[SUBLANES - 1:SUBLANES - 1 + tb, :]
            + convw_ref[2:3, :] * cv)
    cv_ref[0:SUBLANES, :] = cv_ref[tb:tb + SUBLANES, :]
    pre = (b_gate * conv).astype(BF16)

    g_c = proj(o_gc, o_gs)
    y_mem = _dot(o, attnp_ref[...])
    y_conv = _dot(pre, convp_ref[...])
    g_m = proj(o_gm, o_gm + d)
    part_ref[...] = (jax.nn.sigmoid(g_c) * y_conv + jax.nn.sigmoid(g_m) * y_mem).astype(BF16)
    sgs_ref[...] = jax.nn.sigmoid(proj(o_gs, o_gm)).astype(BF16)
    u_ref[...] = proj(o_u, o_q).astype(BF16)


def _mixer(x, g1, w_in, conv_w, conv_p, kt, v, attn_p, *, cw, sw, mw):
    b, s, d = x.shape
    tb = MIXER_ROW_TILE
    m = kt.shape[2]
    kern = functools.partial(_mixer_kernel, cw=cw, sw=sw, mw=mw, d=d)
    return pl.pallas_call(
        kern,
        grid=(b, s // tb),
        in_specs=[pl.BlockSpec((None, tb, d), lambda bi, i: (bi, i, 0)),
                  _const_spec((1, d)),
                  _const_spec(w_in.shape),
                  _const_spec(conv_w.shape),
                  _const_spec(conv_p.shape),
                  pl.BlockSpec((None, mw, m), lambda bi, i: (bi, 0, 0)),
                  pl.BlockSpec((None, m, mw), lambda bi, i: (bi, 0, 0)),
                  _const_spec(attn_p.shape)],
        out_specs=(pl.BlockSpec((None, tb, sw), lambda bi, i: (bi, i, 0)),
                   pl.BlockSpec((None, tb, d), lambda bi, i: (bi, i, 0)),
                   pl.BlockSpec((None, tb, d), lambda bi, i: (bi, i, 0))),
        out_shape=(jax.ShapeDtypeStruct((b, s, sw), BF16),
                   jax.ShapeDtypeStruct((b, s, d), BF16),
                   jax.ShapeDtypeStruct((b, s, d), BF16)),
        scratch_shapes=[pltpu.VMEM((tb + SUBLANES, cw), F32)],
        compiler_params=pltpu.CompilerParams(
            dimension_semantics=("arbitrary", "arbitrary"),
            vmem_limit_bytes=VMEM_LIMIT_BYTES),
        name="mixer",
    )(x, g1, w_in, conv_w, conv_p, kt, v, attn_p)


def _s5_kernel(u_ref, ere_ref, eim_ref, lcre_ref, lcim_ref, fre_ref, fim_ref, tz_ref,
               gw_ref, gb_ref, sp_ref, y_ref,
               utm_ref, x_ref, st_ref, hst_ref, yc_ref, ytm_ref, *, n_state):
    batch, n_t, sw = u_ref.shape
    n_c = n_t // SSM_CHUNK
    n_q = sw // QUAD_CH
    tile = SSM_CHUNK * QUAD_CH
    half = LANES // 2
    lo = lax.broadcasted_iota(jnp.int32, (batch, LANES), 1) < half
    step_i = pl.program_id(0)
    slot_a = step_i % 2
    slot_b = 1 - slot_a

    @pl.when(step_i == 0)
    def _():
        hst_ref[...] = jnp.zeros_like(hst_ref)
        ytm_ref[1] = jnp.zeros(ytm_ref.shape[1:], F32)

    sub_t = POST_ROWS // batch

    def post(sb):
        y = ytm_ref[slot_b, sb * POST_ROWS:(sb + 1) * POST_ROWS, :].reshape(sub_t, batch, sw)
        y = jnp.swapaxes(y, 0, 1).reshape(POST_ROWS, sw)
        y = _gelu_tanh(y)
        y = y * jax.nn.sigmoid(_dot(y.astype(BF16), gw_ref[...]) + gb_ref[...])
        out = _dot(y.astype(BF16), sp_ref[...]).astype(BF16)
        y_ref[:, sb * sub_t:(sb + 1) * sub_t, :] = out.reshape(batch, sub_t, -1)

    assert n_t // sub_t == 4, "stage B is interleaved as four sub-blocks"

    n_pairs = n_q // 2

    def gather(j):
        for c in range(n_c):
            r0 = c * batch
            for kp in range(SSM_CHUNK // 2):
                t0 = (c * SSM_CHUNK + 2 * kp) * batch
                s0 = utm_ref[t0:t0 + batch, j * LANES:(j + 1) * LANES]
                s1 = utm_ref[t0 + batch:t0 + 2 * batch, j * LANES:(j + 1) * LANES]
                col = (2 * j) * tile + kp * LANES
                x_ref[r0:r0 + batch, col:col + LANES] = jnp.where(lo, s0, pltpu.roll(s1, half, 1))
                col += tile
                x_ref[r0:r0 + batch, col:col + LANES] = jnp.where(lo, pltpu.roll(s0, half, 1), s1)

    def chunk_inputs(j):
        for q in (2 * j, 2 * j + 1):
            xq = x_ref[:, q * tile:(q + 1) * tile].astype(BF16)
            st_ref[:, q * tile:(q + 1) * tile] = _dot(xq, ere_ref[q])
            st_ref[:, n_state + q * tile:n_state + (q + 1) * tile] = _dot(xq, eim_ref[q])

    def recurrence(j):
        re = slice(j * 2 * tile, (j + 1) * 2 * tile)
        im = slice(n_state + j * 2 * tile, n_state + (j + 1) * 2 * tile)
        l_re = lcre_ref[:, re]
        l_im = lcim_ref[:, re]
        h_re = hst_ref[:, re]
        h_im = hst_ref[:, im]
        for c in range(n_c):
            rows = slice(c * batch, (c + 1) * batch)
            b_re = st_ref[rows, re]
            b_im = st_ref[rows, im]
            st_ref[rows, re] = h_re
            st_ref[rows, im] = h_im
            h_re, h_im = (l_re * h_re - l_im * h_im + b_re, l_re * h_im + l_im * h_re + b_im)
        hst_ref[:, re] = h_re
        hst_ref[:, im] = h_im

    def chunk_outputs(j):
        for q in (2 * j, 2 * j + 1):
            cols = slice(q * tile, (q + 1) * tile)
            xq = x_ref[:, cols].astype(BF16)
            h_re = st_ref[:, cols].astype(BF16)
            h_im = st_ref[:, n_state + q * tile:n_state + (q + 1) * tile].astype(BF16)
            yc_ref[:, cols] = (_dot(h_re, fre_ref[q]) + _dot(h_im, fim_ref[q])
                               + _dot(xq, tz_ref[q]))

    def scatter(j):
        for c in range(n_c):
            r0 = c * batch
            for rp in range(SSM_CHUNK // 2):
                col = (2 * j) * tile + rp * LANES
                a = yc_ref[r0:r0 + batch, col:col + LANES]
                b = yc_ref[r0:r0 + batch, col + tile:col + tile + LANES]
                t0 = (c * SSM_CHUNK + 2 * rp) * batch
                ytm_ref[slot_a, t0:t0 + batch, j * LANES:(j + 1) * LANES] = jnp.where(
                    lo, a, pltpu.roll(b, half, 1))
                ytm_ref[slot_a, t0 + batch:t0 + 2 * batch, j * LANES:(j + 1) * LANES] = jnp.where(
                    lo, pltpu.roll(a, half, 1), b)

    phases = (gather, chunk_inputs, recurrence, chunk_outputs, scatter)
    assert n_pairs == 4 and n_state == n_q * tile, "the schedule below is written for 4 quad pairs"
    utm_ref[...] = jnp.swapaxes(u_ref[...].astype(F32), 0, 1).reshape(n_t * batch, sw)
    for tick in range(n_pairs + len(phases) - 1):
        for depth, phase in enumerate(phases):
            j = tick - depth
            if 0 <= j < n_pairs:
                phase(j)
        if tick % 2 == 1:
            post(tick // 2)


def _s5(u, ere, eim, lcre, lcim, fre, fim, tz, glu_w, glu_b, ssm_p):
    batch, s, sw = u.shape
    d = ssm_p.shape[1]
    n_state = lcre.shape[1]
    n_t = TIME_TILE
    n_blocks = s // n_t
    rows_c = (n_t // SSM_CHUNK) * batch
    kern = functools.partial(_s5_kernel, n_state=n_state)
    consts = (ere, eim, lcre, lcim, fre, fim, tz, glu_w, glu_b, ssm_p)
    return pl.pallas_call(
        kern,
        grid=(n_blocks + 1,),
        in_specs=[pl.BlockSpec((batch, n_t, sw), lambda i: (0, jnp.minimum(i, n_blocks - 1), 0))]
                 + [_const_spec(c.shape) for c in consts],
        out_specs=pl.BlockSpec((batch, n_t, d), lambda i: (0, jnp.maximum(i - 1, 0), 0)),
        out_shape=jax.ShapeDtypeStruct((batch, s, d), BF16),
        scratch_shapes=[pltpu.VMEM((batch * n_t, sw), F32),
                        pltpu.VMEM((rows_c, SSM_CHUNK * sw), F32),
                        pltpu.VMEM((rows_c, 2 * n_state), F32),
                        pltpu.VMEM((batch, 2 * n_state), F32),
                        pltpu.VMEM((rows_c, SSM_CHUNK * sw), F32),
                        pltpu.VMEM((2, batch * n_t, sw), F32)],
        compiler_params=pltpu.CompilerParams(
            dimension_semantics=("arbitrary",),
            vmem_limit_bytes=VMEM_LIMIT_BYTES),
        name="s5",
    )(u, *consts)


def _ffn_kernel(x_ref, part_ref, sgs_ref, yssm_ref, wo_ref, g2_ref, wg_ref, wu_ref, wd_ref,
                gf_ref, o_ref, *, chunks):
    tb = x_ref.shape[0] // ROW_SUBTILES
    subs = [slice(sub * tb, (sub + 1) * tb) for sub in range(ROW_SUBTILES)]
    x1s = []
    for rows in subs:
        merged = (part_ref[rows, :].astype(F32)
                  + sgs_ref[rows, :].astype(F32) * yssm_ref[rows, :].astype(F32))
        x1s.append(x_ref[rows, :] + _dot(merged.astype(BF16), wo_ref[...]))
    h2s = [_rms(x1, g2_ref[...]).astype(BF16) for x1 in x1s]
    outs = []
    for x1, h2 in zip(x1s, h2s):
        acc = x1
        for lo, hi in chunks:
            g = _dot(h2, wg_ref[:, lo:hi])
            a = (g * jax.nn.sigmoid(g) * _dot(h2, wu_ref[:, lo:hi])).astype(BF16)
            acc = acc + _dot(a, wd_ref[lo:hi, :])
        outs.append(_rms(acc, gf_ref[...]))
    o_ref[...] = jnp.concatenate(outs, axis=0)


def _ffn_chunks(hidden):
    step = 4 * MXU_TILE
    return tuple((lo, min(lo + step, hidden)) for lo in range(0, hidden, step))


def _ffn(x, part, sgs, yssm, w_o, g2, wg, wu, wd, gf):
    b, s, d = x.shape
    tb = ROW_TILE
    kern = functools.partial(_ffn_kernel, chunks=_ffn_chunks(wg.shape[1]))
    row_spec = pl.BlockSpec((None, tb, d), lambda bi, i: (bi, i, 0))
    return pl.pallas_call(
        kern,
        grid=(b, s // tb),
        in_specs=[row_spec, row_spec, row_spec, row_spec,
                  _const_spec(w_o.shape), _const_spec((1, d)),
                  _const_spec(wg.shape), _const_spec(wu.shape), _const_spec(wd.shape),
                  _const_spec((1, d))],
        out_specs=row_spec,
        out_shape=jax.ShapeDtypeStruct((b, s, d), x.dtype),
        compiler_params=pltpu.CompilerParams(
            dimension_semantics=("arbitrary", "arbitrary"),
            vmem_limit_bytes=FFN_VMEM_LIMIT_BYTES),
        name="ffn",
    )(x, part, sgs, yssm, w_o, g2, wg, wu, wd, gf)


def _pack_toeplitz(kq, d_skip, h):
    n, n_q, qc, _ = kq.shape
    idx = jnp.arange(qc)
    same_group = (idx[:, None] // h == idx[None, :] // h).astype(F32)
    blk = kq * same_group
    blk = blk.at[0].add(jnp.eye(qc, dtype=F32) * d_skip.reshape(n_q, 1, qc))
    zero = jnp.zeros_like(blk[0])
    rows = [jnp.concatenate([blk[r - k] if r >= k else zero for r in range(n)], axis=-1)
            for k in range(n)]
    return jnp.concatenate(rows, axis=1)


def kernel(x, mem, norm1_g, w_in, conv_w, conv_proj, ssm_A_re, ssm_A_im, ssm_log_dt, ssm_B_re,
           ssm_B_im, ssm_C_re, ssm_C_im, ssm_D, ssm_glu_w, ssm_glu_b, ssm_proj, mem_norm_g,
           attn_wk, attn_wv, attn_proj, w_o, norm2_g, ffn_w_gate, ffn_w_up, ffn_w_down,
           final_norm_g):
    b, s, d = x.shape
    depth = w_in.shape[0]
    cw = conv_w.shape[2]
    sw = ssm_D.shape[1]
    mw = attn_wk.shape[2]
    n_groups = ssm_A_re.shape[1]
    n_state = n_groups * SSM_STATE
    assert depth == 1, "the final norm is fused into the single layer's ffn kernel"
    assert b == SUBLANES, "the S5 recurrence keeps the batch on the sublane axis"

    for l in range(depth):
        lc_re, lc_im, e_re, e_im, f_re, f_im, kq = _ssm_prep(
            ssm_A_re[l], ssm_A_im[l], ssm_log_dt[l],
            ssm_B_re[l].transpose(0, 2, 1), ssm_B_im[l].transpose(0, 2, 1),
            ssm_C_re[l], ssm_C_im[l])
        ere = e_re.astype(BF16)
        eim = e_im.astype(BF16)
        fre = f_re.transpose(0, 2, 1).astype(BF16)
        fim = f_im.transpose(0, 2, 1).astype(BF16)
        tz = _pack_toeplitz(kq, ssm_D[l], SSM_GROUP).astype(BF16)
        lcre = jnp.broadcast_to(lc_re.reshape(1, n_state), (b, n_state))
        lcim = jnp.broadcast_to(lc_im.reshape(1, n_state), (b, n_state))

        kt, v = _mem_kv(mem, mem_norm_g[l].reshape(1, d), attn_wk[l].T.astype(BF16),
                        attn_wv[l].astype(BF16))
        u, sgs, part = _mixer(
            x, norm1_g[l].reshape(1, d), w_in[l].astype(BF16), conv_w[l],
            conv_proj[l].astype(BF16), kt, v, attn_proj[l].astype(BF16), cw=cw, sw=sw, mw=mw)
        yssm = _s5(u, ere, eim, lcre, lcim, fre, fim, tz,
                   ssm_glu_w[l].astype(BF16), ssm_glu_b[l].reshape(1, sw),
                   ssm_proj[l].astype(BF16))
        x = _ffn(x, part, sgs, yssm, w_o[l].astype(BF16),
                 norm2_g[l].reshape(1, d), ffn_w_gate[l].astype(BF16), ffn_w_up[l].astype(BF16),
                 ffn_w_down[l].astype(BF16), final_norm_g.reshape(1, d))
    return x
```

```python
import functools
import math

import jax
import jax.numpy as jnp
from jax import lax
from jax.experimental import pallas as pl
from jax.experimental.pallas import tpu as pltpu

F32 = jnp.float32
BF16 = jnp.bfloat16

EPS = 1e-6
CONV_K = 3
SSM_GROUP = 16
SSM_STATE = 64
MEM_HEADS = 4
MEM_HEAD_DIM = 128

MXU_TILE = 256
SUBLANES = 8
LANES = 128
VMEM_LIMIT_BYTES = 56 * 1024 * 1024
FFN_VMEM_LIMIT_BYTES = 62 * 1024 * 1024

MIXER_ROW_TILE = 1024
ROW_TILE = 1024
ROW_SUBTILES = 4
TIME_TILE = 256
SSM_CHUNK = 4
QUAD_CH = MXU_TILE // SSM_CHUNK
POST_ROWS = 512


def _rms(x, g):
    ms = jnp.mean(x * x, axis=-1, keepdims=True)
    return x * lax.rsqrt(ms + EPS) * g


def _dot(a, b):
    return jnp.dot(a, b, preferred_element_type=F32)


def _gelu_tanh(x):
    k0 = -2.0 * math.sqrt(2.0 / math.pi) * math.log2(math.e)
    w = x * (k0 + (k0 * 0.044715) * (x * x))
    return x / (1.0 + jnp.exp2(w))


def _const_spec(shape):
    nd = len(shape)
    return pl.BlockSpec(shape, lambda *_: (0,) * nd, pipeline_mode=pl.Buffered(1))


def _ssm_prep_kernel(are_ref, aim_ref, ldt_ref, bre_ref, bim_ref, cre_ref, cim_ref,
                     lcre_ref, lcim_ref, ere_ref, eim_ref, fre_ref, fim_ref, kq_ref):
    a_re = are_ref[...]
    a_im = aim_ref[...]
    dt = jnp.exp(ldt_ref[...])
    mag = jnp.exp(a_re * dt)
    l_re = mag * jnp.cos(a_im * dt)
    l_im = mag * jnp.sin(a_im * dt)
    n_re = l_re - 1.0
    n_im = l_im
    den = a_re * a_re + a_im * a_im
    q_re = ((n_re * a_re + n_im * a_im) / den)[:, None, :]
    q_im = ((n_im * a_re - n_re * a_im) / den)[:, None, :]
    b_re = bre_ref[...]
    b_im = bim_ref[...]
    bb_re = q_re * b_re - q_im * b_im
    bb_im = q_re * b_im + q_im * b_re
    c_re = cre_ref[...]
    c_im = cim_ref[...]
    g, h, p = c_re.shape

    pows = [(jnp.ones_like(l_re), jnp.zeros_like(l_re))]
    for _ in range(SSM_CHUNK):
        pr, pi = pows[-1]
        pows.append((pr * l_re - pi * l_im, pr * l_im + pi * l_re))
    lcre_ref[...] = pows[SSM_CHUNK][0]
    lcim_ref[...] = pows[SSM_CHUNK][1]

    def place(out_ref, blocks):
        per_quad = QUAD_CH // h
        out_ref[...] = jnp.zeros(out_ref.shape, F32)
        for k, blk in enumerate(blocks):
            for gi in range(g):
                a = gi % per_quad
                r0 = (k * per_quad + a) * h
                out_ref[gi // per_quad, r0:r0 + h, a * p:(a + 1) * p] = blk[gi]

    e_re, e_im, f_re, f_im = [], [], [], []
    for k in range(SSM_CHUNK):
        pr, pi = pows[SSM_CHUNK - 1 - k]
        pr, pi = pr[:, None, :], pi[:, None, :]
        e_re.append(pr * bb_re - pi * bb_im)
        e_im.append(pr * bb_im + pi * bb_re)
    for r in range(SSM_CHUNK):
        pr, pi = pows[r + 1]
        pr, pi = pr[:, None, :], pi[:, None, :]
        f_re.append(c_re * pr - c_im * pi)
        f_im.append(-(c_re * pi + c_im * pr))
    place(ere_ref, e_re)
    place(eim_ref, e_im)
    place(fre_ref, f_re)
    place(fim_ref, f_im)
    nt = (((1,), (1,)), ((), ()))
    c_re2 = c_re.reshape(g * h, p)
    c_im2 = c_im.reshape(g * h, p)
    for j in range(SSM_CHUNK):
        pr, pi = pows[j]
        pr, pi = pr[:, None, :], pi[:, None, :]
        w_re = (pr * bb_re - pi * bb_im).reshape(g * h, p)
        w_im = (pr * bb_im + pi * bb_re).reshape(g * h, p)
        kk = (lax.dot_general(w_re, c_re2, nt, precision=lax.Precision.HIGHEST,
                              preferred_element_type=F32)
              - lax.dot_general(w_im, c_im2, nt, precision=lax.Precision.HIGHEST,
                                preferred_element_type=F32))
        for q in range(g * h // QUAD_CH):
            kq_ref[j, q] = kk[q * QUAD_CH:(q + 1) * QUAD_CH, q * QUAD_CH:(q + 1) * QUAD_CH]


def _ssm_prep(a_re, a_im, log_dt, b_re_t, b_im_t, c_re, c_im):
    g, p = a_re.shape
    h = b_re_t.shape[1]
    n = SSM_CHUNK
    return pl.pallas_call(
        _ssm_prep_kernel,
        out_shape=(jax.ShapeDtypeStruct((g, p), F32), jax.ShapeDtypeStruct((g, p), F32),
                   *[jax.ShapeDtypeStruct((g * h // QUAD_CH, n * QUAD_CH, n * QUAD_CH), F32)] * 4,
                   jax.ShapeDtypeStruct((n, g * h // QUAD_CH, QUAD_CH, QUAD_CH), F32)),
        name="ssm_prep",
    )(a_re, a_im, log_dt.reshape(g, 1), b_re_t, b_im_t, c_re, c_im)


def _mem_kv_kernel(mem_ref, g_ref, wkt_ref, wv_ref, kt_ref, v_ref):
    mn = _rms(mem_ref[...], g_ref[...]).astype(BF16)
    kt = lax.dot_general(wkt_ref[...], mn, (((1,), (1,)), ((), ())),
                         preferred_element_type=F32)
    kt_ref[...] = kt.astype(BF16)
    v_ref[...] = _dot(mn, wv_ref[...]).astype(BF16)


def _mem_kv(mem, g, wkt, wv):
    b, m, d = mem.shape
    w = wv.shape[1]
    return pl.pallas_call(
        _mem_kv_kernel,
        grid=(b,),
        in_specs=[pl.BlockSpec((None, m, d), lambda i: (i, 0, 0)),
                  pl.BlockSpec((1, d), lambda i: (0, 0)),
                  pl.BlockSpec((w, d), lambda i: (0, 0)),
                  pl.BlockSpec((d, w), lambda i: (0, 0))],
        out_specs=(pl.BlockSpec((None, w, m), lambda i: (i, 0, 0)),
                   pl.BlockSpec((None, m, w), lambda i: (i, 0, 0))),
        out_shape=(jax.ShapeDtypeStruct((b, w, m), BF16), jax.ShapeDtypeStruct((b, m, w), BF16)),
        name="mem_kv",
    )(mem, g, wkt, wv)


def _mixer_kernel(x_ref, g1_ref, win_ref, convw_ref, convp_ref, kt_ref, v_ref, attnp_ref,
                  u_ref, sgs_ref, part_ref, cv_ref, *, cw, sw, mw, d):
    i = pl.program_id(1)
    tb = x_ref.shape[0]
    h = _rms(x_ref[...], g1_ref[...]).astype(BF16)

    def proj(lo, hi):
        return _dot(h, win_ref[:, lo:hi])

    o_u = 3 * cw
    o_q = o_u + sw
    o_gc = o_q + mw
    o_gs = o_gc + d
    o_gm = o_gs + d

    q = proj(o_q, o_gc)
    scale = MEM_HEAD_DIM ** -0.5
    scores = []
    for hd in range(MEM_HEADS):
        lo = hd * MEM_HEAD_DIM
        scores.append(_dot(q[:, lo:lo + MEM_HEAD_DIM].astype(BF16),
                           kt_ref[lo:lo + MEM_HEAD_DIM, :]) * scale)
    c_gate = proj(cw, 2 * cw)
    v_conv = proj(2 * cw, 3 * cw)

    heads = []
    for hd in range(MEM_HEADS):
        lo = hd * MEM_HEAD_DIM
        s = scores[hd]
        e = jnp.exp(s - jnp.max(s, axis=-1, keepdims=True))
        pv = _dot(e.astype(BF16), v_ref[:, lo:lo + MEM_HEAD_DIM])
        heads.append(pv / jnp.sum(e, axis=-1, keepdims=True))
    o = jnp.concatenate(heads, axis=-1).astype(BF16)
    b_gate = proj(0, cw)

    cv = c_gate * v_conv

    @pl.when(i == 0)
    def _():
        cv_ref[0:SUBLANES, :] = jnp.zeros((SUBLANES, cw), F32)

    cv_ref[SUBLANES:SUBLANES + tb, :] = cv
    conv = (convw_ref[0:1, :] * cv_ref[SUBLANES - 2:SUBLANES - 2 + tb, :]
            + convw_ref[1:2, :] * cv_ref[SUBLANES - 1:SUBLANES - 1 + tb, :]
            + convw_ref[2:3, :] * cv)
    cv_ref[0:SUBLANES, :] = cv_ref[tb:tb + SUBLANES, :]
    pre = (b_gate * conv).astype(BF16)

    g_c = proj(o_gc, o_gs)
    y_mem = _dot(o, attnp_ref[...])
    y_conv = _dot(pre, convp_ref[...])
    g_m = proj(o_gm, o_gm + d)
    part_ref[...] = (jax.nn.sigmoid(g_c) * y_conv + jax.nn.sigmoid(g_m) * y_mem).astype(BF16)
    sgs_ref[...] = jax.nn.sigmoid(proj(o_gs, o_gm)).astype(BF16)
    u_ref[...] = proj(o_u, o_q).astype(BF16)


def _mixer(x, g1, w_in, conv_w, conv_p, kt, v, attn_p, *, cw, sw, mw):
    b, s, d = x.shape
    tb = MIXER_ROW_TILE
    m = kt.shape[2]
    kern = functools.partial(_mixer_kernel, cw=cw, sw=sw, mw=mw, d=d)
    return pl.pallas_call(
        kern,
        grid=(b, s // tb),
        in_specs=[pl.BlockSpec((None, tb, d), lambda bi, i: (bi, i, 0)),
                  _const_spec((1, d)),
                  _const_spec(w_in.shape),
                  _const_spec(conv_w.shape),
                  _const_spec(conv_p.shape),
                  pl.BlockSpec((None, mw, m), lambda bi, i: (bi, 0, 0)),
                  pl.BlockSpec((None, m, mw), lambda bi, i: (bi, 0, 0)),
                  _const_spec(attn_p.shape)],
        out_specs=(pl.BlockSpec((None, tb, sw), lambda bi, i: (bi, i, 0)),
                   pl.BlockSpec((None, tb, d), lambda bi, i: (bi, i, 0)),
                   pl.BlockSpec((None, tb, d), lambda bi, i: (bi, i, 0))),
        out_shape=(jax.ShapeDtypeStruct((b, s, sw), BF16),
                   jax.ShapeDtypeStruct((b, s, d), BF16),
                   jax.ShapeDtypeStruct((b, s, d), BF16)),
        scratch_shapes=[pltpu.VMEM((tb + SUBLANES, cw), F32)],
        compiler_params=pltpu.CompilerParams(
            dimension_semantics=("arbitrary", "arbitrary"),
            vmem_limit_bytes=VMEM_LIMIT_BYTES),
        name="mixer",
    )(x, g1, w_in, conv_w, conv_p, kt, v, attn_p)


def _s5_kernel(u_ref, ere_ref, eim_ref, lcre_ref, lcim_ref, fre_ref, fim_ref, tz_ref,
               gw_ref, gb_ref, sp_ref, y_ref,
               utm_ref, x_ref, st_ref, hst_ref, yc_ref, ytm_ref, *, n_state):
    batch, n_t, sw = u_ref.shape
    n_c = n_t // SSM_CHUNK
    n_q = sw // QUAD_CH
    tile = SSM_CHUNK * QUAD_CH
    half = LANES // 2
    lo = lax.broadcasted_iota(jnp.int32, (batch, LANES), 1) < half
    step_i = pl.program_id(0)
    slot_a = step_i % 2
    slot_b = 1 - slot_a

    @pl.when(step_i == 0)
    def _():
        hst_ref[...] = jnp.zeros_like(hst_ref)
        ytm_ref[1] = jnp.zeros(ytm_ref.shape[1:], F32)

    sub_t = POST_ROWS // batch

    def post(sb):
        y = ytm_ref[slot_b, sb * POST_ROWS:(sb + 1) * POST_ROWS, :].reshape(sub_t, batch, sw)
        y = jnp.swapaxes(y, 0, 1).reshape(POST_ROWS, sw)
        y = y * jax.nn.sigmoid(_dot(y.astype(BF16), gw_ref[...]) + gb_ref[...])
        out = _dot(y.astype(BF16), sp_ref[...]).astype(BF16)
        y_ref[:, sb * sub_t:(sb + 1) * sub_t, :] = out.reshape(batch, sub_t, -1)

    assert n_t // sub_t == 4, "stage B is interleaved as four sub-blocks"

    n_pairs = n_q // 2

    def gather(j):
        for c in range(n_c):
            r0 = c * batch
            for kp in range(SSM_CHUNK // 2):
                t0 = (c * SSM_CHUNK + 2 * kp) * batch
                s0 = utm_ref[t0:t0 + batch, j * LANES:(j + 1) * LANES]
                s1 = utm_ref[t0 + batch:t0 + 2 * batch, j * LANES:(j + 1) * LANES]
                col = (2 * j) * tile + kp * LANES
                x_ref[r0:r0 + batch, col:col + LANES] = jnp.where(lo, s0, pltpu.roll(s1, half, 1))
                col += tile
                x_ref[r0:r0 + batch, col:col + LANES] = jnp.where(lo, pltpu.roll(s0, half, 1), s1)

    def chunk_inputs(j):
        for q in (2 * j, 2 * j + 1):
            xq = x_ref[:, q * tile:(q + 1) * tile].astype(BF16)
            st_ref[:, q * tile:(q + 1) * tile] = _dot(xq, ere_ref[q])
            st_ref[:, n_state + q * tile:n_state + (q + 1) * tile] = _dot(xq, eim_ref[q])

    def recurrence(j):
        re = slice(j * 2 * tile, (j + 1) * 2 * tile)
        im = slice(n_state + j * 2 * tile, n_state + (j + 1) * 2 * tile)
        l_re = lcre_ref[:, re]
        l_im = lcim_ref[:, re]
        h_re = hst_ref[:, re]
        h_im = hst_ref[:, im]
        for c in range(n_c):
            rows = slice(c * batch, (c + 1) * batch)
            b_re = st_ref[rows, re]
            b_im = st_ref[rows, im]
            st_ref[rows, re] = h_re
            st_ref[rows, im] = h_im
            h_re, h_im = (l_re * h_re - l_im * h_im + b_re, l_re * h_im + l_im * h_re + b_im)
        hst_ref[:, re] = h_re
        hst_ref[:, im] = h_im

    def chunk_outputs(j):
        for q in (2 * j, 2 * j + 1):
            cols = slice(q * tile, (q + 1) * tile)
            xq = x_ref[:, cols].astype(BF16)
            h_re = st_ref[:, cols].astype(BF16)
            h_im = st_ref[:, n_state + q * tile:n_state + (q + 1) * tile].astype(BF16)
            yc_ref[:, cols] = (_dot(h_re, fre_ref[q]) + _dot(h_im, fim_ref[q])
                               + _dot(xq, tz_ref[q]))

    def scatter(j):
        for c in range(n_c):
            r0 = c * batch
            for rp in range(SSM_CHUNK // 2):
                col = (2 * j) * tile + rp * LANES
                a = _gelu_tanh(yc_ref[r0:r0 + batch, col:col + LANES])
                b = _gelu_tanh(yc_ref[r0:r0 + batch, col + tile:col + tile + LANES])
                t0 = (c * SSM_CHUNK + 2 * rp) * batch
                ytm_ref[slot_a, t0:t0 + batch, j * LANES:(j + 1) * LANES] = jnp.where(
                    lo, a, pltpu.roll(b, half, 1))
                ytm_ref[slot_a, t0 + batch:t0 + 2 * batch, j * LANES:(j + 1) * LANES] = jnp.where(
                    lo, pltpu.roll(a, half, 1), b)

    phases = (gather, chunk_inputs, recurrence, chunk_outputs, scatter)
    assert n_pairs == 4 and n_state == n_q * tile, "the schedule below is written for 4 quad pairs"
    utm_ref[...] = jnp.swapaxes(u_ref[...].astype(F32), 0, 1).reshape(n_t * batch, sw)
    for tick in range(n_pairs + len(phases) - 1):
        for depth, phase in enumerate(phases):
            j = tick - depth
            if 0 <= j < n_pairs:
                phase(j)
        if tick % 2 == 1:
            post(tick // 2)


def _s5(u, ere, eim, lcre, lcim, fre, fim, tz, glu_w, glu_b, ssm_p):
    batch, s, sw = u.shape
    d = ssm_p.shape[1]
    n_state = lcre.shape[1]
    n_t = TIME_TILE
    n_blocks = s // n_t
    rows_c = (n_t // SSM_CHUNK) * batch
    kern = functools.partial(_s5_kernel, n_state=n_state)
    consts = (ere, eim, lcre, lcim, fre, fim, tz, glu_w, glu_b, ssm_p)
    return pl.pallas_call(
        kern,
        grid=(n_blocks + 1,),
        in_specs=[pl.BlockSpec((batch, n_t, sw), lambda i: (0, jnp.minimum(i, n_blocks - 1), 0))]
                 + [_const_spec(c.shape) for c in consts],
        out_specs=pl.BlockSpec((batch, n_t, d), lambda i: (0, jnp.maximum(i - 1, 0), 0)),
        out_shape=jax.ShapeDtypeStruct((batch, s, d), BF16),
        scratch_shapes=[pltpu.VMEM((batch * n_t, sw), F32),
                        pltpu.VMEM((rows_c, SSM_CHUNK * sw), F32),
                        pltpu.VMEM((rows_c, 2 * n_state), F32),
                        pltpu.VMEM((batch, 2 * n_state), F32),
                        pltpu.VMEM((rows_c, SSM_CHUNK * sw), F32),
                        pltpu.VMEM((2, batch * n_t, sw), F32)],
        compiler_params=pltpu.CompilerParams(
            dimension_semantics=("arbitrary",),
            vmem_limit_bytes=VMEM_LIMIT_BYTES),
        name="s5",
    )(u, *consts)


def _ffn_kernel(x_ref, part_ref, sgs_ref, yssm_ref, wo_ref, g2_ref, wg_ref, wu_ref, wd_ref,
                gf_ref, o_ref, *, chunks):
    tb = x_ref.shape[0] // ROW_SUBTILES
    subs = [slice(sub * tb, (sub + 1) * tb) for sub in range(ROW_SUBTILES)]
    x1s = []
    for rows in subs:
        merged = (part_ref[rows, :].astype(F32)
                  + sgs_ref[rows, :].astype(F32) * yssm_ref[rows, :].astype(F32))
        x1s.append(x_ref[rows, :] + _dot(merged.astype(BF16), wo_ref[...]))
    h2s = [_rms(x1, g2_ref[...]).astype(BF16) for x1 in x1s]
    outs = []
    for x1, h2 in zip(x1s, h2s):
        acc = x1
        for lo, hi in chunks:
            g = _dot(h2, wg_ref[:, lo:hi])
            a = (g * jax.nn.sigmoid(g) * _dot(h2, wu_ref[:, lo:hi])).astype(BF16)
            acc = acc + _dot(a, wd_ref[lo:hi, :])
        outs.append(_rms(acc, gf_ref[...]))
    o_ref[...] = jnp.concatenate(outs, axis=0)


def _ffn_chunks(hidden):
    step = 4 * MXU_TILE
    return tuple((lo, min(lo + step, hidden)) for lo in range(0, hidden, step))


def _ffn(x, part, sgs, yssm, w_o, g2, wg, wu, wd, gf):
    b, s, d = x.shape
    tb = ROW_TILE
    kern = functools.partial(_ffn_kernel, chunks=_ffn_chunks(wg.shape[1]))
    row_spec = pl.BlockSpec((None, tb, d), lambda bi, i: (bi, i, 0))
    return pl.pallas_call(
        kern,
        grid=(b, s // tb),
        in_specs=[row_spec, row_spec, row_spec, row_spec,
                  _const_spec(w_o.shape), _const_spec((1, d)),
                  _const_spec(wg.shape), _const_spec(wu.shape), _const_spec(wd.shape),
                  _const_spec((1, d))],
        out_specs=row_spec,
        out_shape=jax.ShapeDtypeStruct((b, s, d), x.dtype),
        compiler_params=pltpu.CompilerParams(
            dimension_semantics=("arbitrary", "arbitrary"),
            vmem_limit_bytes=FFN_VMEM_LIMIT_BYTES),
        name="ffn",
    )(x, part, sgs, yssm, w_o, g2, wg, wu, wd, gf)


def _pack_toeplitz(kq, d_skip, h):
    n, n_q, qc, _ = kq.shape
    idx = jnp.arange(qc)
    same_group = (idx[:, None] // h == idx[None, :] // h).astype(F32)
    blk = kq * same_group
    blk = blk.at[0].add(jnp.eye(qc, dtype=F32) * d_skip.reshape(n_q, 1, qc))
    zero = jnp.zeros_like(blk[0])
    rows = [jnp.concatenate([blk[r - k] if r >= k else zero for r in range(n)], axis=-1)
            for k in range(n)]
    return jnp.concatenate(rows, axis=1)


def kernel(x, mem, norm1_g, w_in, conv_w, conv_proj, ssm_A_re, ssm_A_im, ssm_log_dt, ssm_B_re,
           ssm_B_im, ssm_C_re, ssm_C_im, ssm_D, ssm_glu_w, ssm_glu_b, ssm_proj, mem_norm_g,
           attn_wk, attn_wv, attn_proj, w_o, norm2_g, ffn_w_gate, ffn_w_up, ffn_w_down,
           final_norm_g):
    b, s, d = x.shape
    depth = w_in.shape[0]
    cw = conv_w.shape[2]
    sw = ssm_D.shape[1]
    mw = attn_wk.shape[2]
    n_groups = ssm_A_re.shape[1]
    n_state = n_groups * SSM_STATE
    assert depth == 1, "the final norm is fused into the single layer's ffn kernel"
    assert b == SUBLANES, "the S5 recurrence keeps the batch on the sublane axis"

    for l in range(depth):
        lc_re, lc_im, e_re, e_im, f_re, f_im, kq = _ssm_prep(
            ssm_A_re[l], ssm_A_im[l], ssm_log_dt[l],
            ssm_B_re[l].transpose(0, 2, 1), ssm_B_im[l].transpose(0, 2, 1),
            ssm_C_re[l], ssm_C_im[l])
        ere = e_re.astype(BF16)
        eim = e_im.astype(BF16)
        fre = f_re.transpose(0, 2, 1).astype(BF16)
        fim = f_im.transpose(0, 2, 1).astype(BF16)
        tz = _pack_toeplitz(kq, ssm_D[l], SSM_GROUP).astype(BF16)
        lcre = jnp.broadcast_to(lc_re.reshape(1, n_state), (b, n_state))
        lcim = jnp.broadcast_to(lc_im.reshape(1, n_state), (b, n_state))

        kt, v = _mem_kv(mem, mem_norm_g[l].reshape(1, d), attn_wk[l].T.astype(BF16),
                        attn_wv[l].astype(BF16))
        u, sgs, part = _mixer(
            x, norm1_g[l].reshape(1, d), w_in[l].astype(BF16), conv_w[l],
            conv_proj[l].astype(BF16), kt, v, attn_proj[l].astype(BF16), cw=cw, sw=sw, mw=mw)
        yssm = _s5(u, ere, eim, lcre, lcim, fre, fim, tz,
                   ssm_glu_w[l].astype(BF16), ssm_glu_b[l].reshape(1, sw),
                   ssm_proj[l].astype(BF16))
        x = _ffn(x, part, sgs, yssm, w_o[l].astype(BF16),
                 norm2_g[l].reshape(1, d), ffn_w_gate[l].astype(BF16), ffn_w_up[l].astype(BF16),
                 ffn_w_down[l].astype(BF16), final_norm_g.reshape(1, d))
    return x
```

```python
import functools
import math

import jax
import jax.numpy as jnp
from jax import lax
from jax.experimental import pallas as pl
from jax.experimental.pallas import tpu as pltpu

F32 = jnp.float32
BF16 = jnp.bfloat16

EPS = 1e-6
SSM_GROUP = 16
SSM_STATE = 64
MEM_HEADS = 4
MEM_HEAD_DIM = 128

MXU_TILE = 256
SUBLANES = 8
LANES = 128
VMEM_LIMIT_BYTES = 56 * 1024 * 1024
FFN_VMEM_LIMIT_BYTES = 62 * 1024 * 1024

MIXER_ROW_TILE = 1024
FFN_ROW_TILE = 1024
FFN_SUBTILES = 4
TIME_TILE = 256
SSM_CHUNK = 4
QUAD_CH = MXU_TILE // SSM_CHUNK
POST_ROWS = 512


def _rms(x, g):
    ms = jnp.mean(x * x, axis=-1, keepdims=True)
    return x * lax.rsqrt(ms + EPS) * g


def _dot(a, b):
    return jnp.dot(a, b, preferred_element_type=F32)


def _gelu_tanh(x):
    k0 = -2.0 * math.sqrt(2.0 / math.pi) * math.log2(math.e)
    w = x * (k0 + (k0 * 0.044715) * (x * x))
    return x / (1.0 + jnp.exp2(w))


def _const_spec(shape):
    nd = len(shape)
    return pl.BlockSpec(shape, lambda *_: (0,) * nd, pipeline_mode=pl.Buffered(1))


def _ssm_prep_kernel(are_ref, aim_ref, ldt_ref, bre_ref, bim_ref, cre_ref, cim_ref,
                     lcre_ref, lcim_ref, ere_ref, eim_ref, fre_ref, fim_ref, kq_ref):
    a_re = are_ref[...]
    a_im = aim_ref[...]
    dt = jnp.exp(ldt_ref[...])
    mag = jnp.exp(a_re * dt)
    l_re = mag * jnp.cos(a_im * dt)
    l_im = mag * jnp.sin(a_im * dt)
    n_re = l_re - 1.0
    n_im = l_im
    den = a_re * a_re + a_im * a_im
    q_re = ((n_re * a_re + n_im * a_im) / den)[:, None, :]
    q_im = ((n_im * a_re - n_re * a_im) / den)[:, None, :]
    b_re = bre_ref[...]
    b_im = bim_ref[...]
    bb_re = q_re * b_re - q_im * b_im
    bb_im = q_re * b_im + q_im * b_re
    c_re = cre_ref[...]
    c_im = cim_ref[...]
    g, h, p = c_re.shape

    pows = [(jnp.ones_like(l_re), jnp.zeros_like(l_re))]
    for _ in range(SSM_CHUNK):
        pr, pi = pows[-1]
        pows.append((pr * l_re - pi * l_im, pr * l_im + pi * l_re))
    lcre_ref[...] = pows[SSM_CHUNK][0]
    lcim_ref[...] = pows[SSM_CHUNK][1]

    def place(out_ref, blocks):
        per_quad = QUAD_CH // h
        out_ref[...] = jnp.zeros(out_ref.shape, F32)
        for k, blk in enumerate(blocks):
            for gi in range(g):
                a = gi % per_quad
                r0 = (k * per_quad + a) * h
                out_ref[gi // per_quad, r0:r0 + h, a * p:(a + 1) * p] = blk[gi]

    e_re, e_im, f_re, f_im = [], [], [], []
    for k in range(SSM_CHUNK):
        pr, pi = pows[SSM_CHUNK - 1 - k]
        pr, pi = pr[:, None, :], pi[:, None, :]
        e_re.append(pr * bb_re - pi * bb_im)
        e_im.append(pr * bb_im + pi * bb_re)
    for r in range(SSM_CHUNK):
        pr, pi = pows[r + 1]
        pr, pi = pr[:, None, :], pi[:, None, :]
        f_re.append(c_re * pr - c_im * pi)
        f_im.append(-(c_re * pi + c_im * pr))
    place(ere_ref, e_re)
    place(eim_ref, e_im)
    place(fre_ref, f_re)
    place(fim_ref, f_im)
    nt = (((1,), (1,)), ((), ()))
    c_re2 = c_re.reshape(g * h, p)
    c_im2 = c_im.reshape(g * h, p)
    for j in range(SSM_CHUNK):
        pr, pi = pows[j]
        pr, pi = pr[:, None, :], pi[:, None, :]
        w_re = (pr * bb_re - pi * bb_im).reshape(g * h, p)
        w_im = (pr * bb_im + pi * bb_re).reshape(g * h, p)
        kk = (lax.dot_general(w_re, c_re2, nt, precision=lax.Precision.HIGHEST,
                              preferred_element_type=F32)
              - lax.dot_general(w_im, c_im2, nt, precision=lax.Precision.HIGHEST,
                                preferred_element_type=F32))
        for q in range(g * h // QUAD_CH):
            kq_ref[j, q] = kk[q * QUAD_CH:(q + 1) * QUAD_CH, q * QUAD_CH:(q + 1) * QUAD_CH]


def _ssm_prep(a_re, a_im, log_dt, b_re_t, b_im_t, c_re, c_im):
    g, p = a_re.shape
    h = b_re_t.shape[1]
    n = SSM_CHUNK
    return pl.pallas_call(
        _ssm_prep_kernel,
        out_shape=(jax.ShapeDtypeStruct((g, p), F32), jax.ShapeDtypeStruct((g, p), F32),
                   *[jax.ShapeDtypeStruct((g * h // QUAD_CH, n * QUAD_CH, n * QUAD_CH), F32)] * 4,
                   jax.ShapeDtypeStruct((n, g * h // QUAD_CH, QUAD_CH, QUAD_CH), F32)),
        name="ssm_prep",
    )(a_re, a_im, log_dt.reshape(g, 1), b_re_t, b_im_t, c_re, c_im)


def _mem_kv_kernel(mem_ref, g_ref, wkt_ref, wv_ref, kt_ref, v_ref):
    mn = _rms(mem_ref[...], g_ref[...]).astype(BF16)
    kt = lax.dot_general(wkt_ref[...], mn, (((1,), (1,)), ((), ())),
                         preferred_element_type=F32)
    kt_ref[...] = kt.astype(BF16)
    v_ref[...] = _dot(mn, wv_ref[...]).astype(BF16)


def _mem_kv(mem, g, wkt, wv):
    b, m, d = mem.shape
    w = wv.shape[1]
    return pl.pallas_call(
        _mem_kv_kernel,
        grid=(b,),
        in_specs=[pl.BlockSpec((None, m, d), lambda i: (i, 0, 0)),
                  pl.BlockSpec((1, d), lambda i: (0, 0)),
                  pl.BlockSpec((w, d), lambda i: (0, 0)),
                  pl.BlockSpec((d, w), lambda i: (0, 0))],
        out_specs=(pl.BlockSpec((None, w, m), lambda i: (i, 0, 0)),
                   pl.BlockSpec((None, m, w), lambda i: (i, 0, 0))),
        out_shape=(jax.ShapeDtypeStruct((b, w, m), BF16), jax.ShapeDtypeStruct((b, m, w), BF16)),
        name="mem_kv",
    )(mem, g, wkt, wv)


def _mixer_kernel(x_ref, g1_ref, win_ref, convw_ref, convp_ref, kt_ref, v_ref, attnp_ref,
                  u_ref, sgs_ref, part_ref, cv_ref, *, cw, sw, mw, d):
    i = pl.program_id(1)
    tb = x_ref.shape[0]
    h = _rms(x_ref[...], g1_ref[...]).astype(BF16)

    def proj(lo, hi):
        return _dot(h, win_ref[:, lo:hi])

    o_u = 3 * cw
    o_q = o_u + sw
    o_gc = o_q + mw
    o_gs = o_gc + d
    o_gm = o_gs + d

    q = proj(o_q, o_gc)
    scale = MEM_HEAD_DIM ** -0.5
    scores = []
    for hd in range(MEM_HEADS):
        lo = hd * MEM_HEAD_DIM
        scores.append(_dot(q[:, lo:lo + MEM_HEAD_DIM].astype(BF16),
                           kt_ref[lo:lo + MEM_HEAD_DIM, :]) * scale)
    c_gate = proj(cw, 2 * cw)
    v_conv = proj(2 * cw, 3 * cw)

    heads = []
    for hd in range(MEM_HEADS):
        lo = hd * MEM_HEAD_DIM
        s = scores[hd]
        e = jnp.exp(s - jnp.max(s, axis=-1, keepdims=True))
        pv = _dot(e.astype(BF16), v_ref[:, lo:lo + MEM_HEAD_DIM])
        heads.append(pv / jnp.sum(e, axis=-1, keepdims=True))
    o = jnp.concatenate(heads, axis=-1).astype(BF16)
    b_gate = proj(0, cw)

    cv = c_gate * v_conv

    @pl.when(i == 0)
    def _():
        cv_ref[0:SUBLANES, :] = jnp.zeros((SUBLANES, cw), F32)

    cv_ref[SUBLANES:SUBLANES + tb, :] = cv
    conv = (convw_ref[0:1, :] * cv_ref[SUBLANES - 2:SUBLANES - 2 + tb, :]
            + convw_ref[1:2, :] * cv_ref[SUBLANES - 1:SUBLANES - 1 + tb, :]
            + convw_ref[2:3, :] * cv)
    cv_ref[0:SUBLANES, :] = cv_ref[tb:tb + SUBLANES, :]
    pre = (b_gate * conv).astype(BF16)

    g_c = proj(o_gc, o_gs)
    y_mem = _dot(o, attnp_ref[...])
    y_conv = _dot(pre, convp_ref[...])
    g_m = proj(o_gm, o_gm + d)
    part_ref[...] = (jax.nn.sigmoid(g_c) * y_conv + jax.nn.sigmoid(g_m) * y_mem).astype(BF16)
    sgs_ref[...] = jax.nn.sigmoid(proj(o_gs, o_gm)).astype(BF16)
    u_ref[...] = proj(o_u, o_q).astype(BF16)


def _mixer(x, g1, w_in, conv_w, conv_p, kt, v, attn_p, *, cw, sw, mw):
    b, s, d = x.shape
    tb = MIXER_ROW_TILE
    m = kt.shape[2]
    kern = functools.partial(_mixer_kernel, cw=cw, sw=sw, mw=mw, d=d)
    return pl.pallas_call(
        kern,
        grid=(b, s // tb),
        in_specs=[pl.BlockSpec((None, tb, d), lambda bi, i: (bi, i, 0)),
                  _const_spec((1, d)),
                  _const_spec(w_in.shape),
                  _const_spec(conv_w.shape),
                  _const_spec(conv_p.shape),
                  pl.BlockSpec((None, mw, m), lambda bi, i: (bi, 0, 0)),
                  pl.BlockSpec((None, m, mw), lambda bi, i: (bi, 0, 0)),
                  _const_spec(attn_p.shape)],
        out_specs=(pl.BlockSpec((None, tb, sw), lambda bi, i: (bi, i, 0)),
                   pl.BlockSpec((None, tb, d), lambda bi, i: (bi, i, 0)),
                   pl.BlockSpec((None, tb, d), lambda bi, i: (bi, i, 0))),
        out_shape=(jax.ShapeDtypeStruct((b, s, sw), BF16),
                   jax.ShapeDtypeStruct((b, s, d), BF16),
                   jax.ShapeDtypeStruct((b, s, d), BF16)),
        scratch_shapes=[pltpu.VMEM((tb + SUBLANES, cw), F32)],
        compiler_params=pltpu.CompilerParams(
            dimension_semantics=("arbitrary", "arbitrary"),
            vmem_limit_bytes=VMEM_LIMIT_BYTES),
        name="mixer",
    )(x, g1, w_in, conv_w, conv_p, kt, v, attn_p)


def _s5_kernel(u_ref, ere_ref, eim_ref, lcre_ref, lcim_ref, fre_ref, fim_ref, tz_ref,
               gw_ref, gb_ref, sp_ref, y_ref,
               utm_ref, x_ref, st_ref, hst_ref, yc_ref, ytm_ref, *, n_state):
    batch, n_t, sw = u_ref.shape
    n_c = n_t // SSM_CHUNK
    n_q = sw // QUAD_CH
    tile = SSM_CHUNK * QUAD_CH
    half = LANES // 2
    lo = lax.broadcasted_iota(jnp.int32, (batch, LANES), 1) < half
    step_i = pl.program_id(0)
    slot_a = step_i % 2
    slot_b = 1 - slot_a

    @pl.when(step_i == 0)
    def _():
        hst_ref[...] = jnp.zeros_like(hst_ref)
        ytm_ref[1] = jnp.zeros(ytm_ref.shape[1:], F32)

    sub_t = POST_ROWS // batch

    def post(sb):
        y = ytm_ref[slot_b, sb * POST_ROWS:(sb + 1) * POST_ROWS, :].reshape(sub_t, batch, sw)
        y = jnp.swapaxes(y, 0, 1).reshape(POST_ROWS, sw)
        y = y * jax.nn.sigmoid(_dot(y.astype(BF16), gw_ref[...]) + gb_ref[...])
        out = _dot(y.astype(BF16), sp_ref[...]).astype(BF16)
        y_ref[:, sb * sub_t:(sb + 1) * sub_t, :] = out.reshape(batch, sub_t, -1)

    assert n_t // sub_t == 4, "stage B is interleaved as four sub-blocks"

    n_pairs = n_q // 2

    def gather(j):
        for c in range(n_c):
            r0 = c * batch
            for kp in range(SSM_CHUNK // 2):
                t0 = (c * SSM_CHUNK + 2 * kp) * batch
                s0 = utm_ref[t0:t0 + batch, j * LANES:(j + 1) * LANES]
                s1 = utm_ref[t0 + batch:t0 + 2 * batch, j * LANES:(j + 1) * LANES]
                col = (2 * j) * tile + kp * LANES
                x_ref[r0:r0 + batch, col:col + LANES] = jnp.where(lo, s0, pltpu.roll(s1, half, 1))
                col += tile
                x_ref[r0:r0 + batch, col:col + LANES] = jnp.where(lo, pltpu.roll(s0, half, 1), s1)

    def chunk_inputs(j):
        for q in (2 * j, 2 * j + 1):
            xq = x_ref[:, q * tile:(q + 1) * tile].astype(BF16)
            st_ref[:, q * tile:(q + 1) * tile] = _dot(xq, ere_ref[q])
            st_ref[:, n_state + q * tile:n_state + (q + 1) * tile] = _dot(xq, eim_ref[q])

    def recurrence(j):
        re = slice(j * 2 * tile, (j + 1) * 2 * tile)
        im = slice(n_state + j * 2 * tile, n_state + (j + 1) * 2 * tile)
        l_re = lcre_ref[:, re]
        l_im = lcim_ref[:, re]
        h_re = hst_ref[:, re]
        h_im = hst_ref[:, im]
        for c in range(n_c):
            rows = slice(c * batch, (c + 1) * batch)
            b_re = st_ref[rows, re]
            b_im = st_ref[rows, im]
            st_ref[rows, re] = h_re
            st_ref[rows, im] = h_im
            h_re, h_im = (l_re * h_re - l_im * h_im + b_re, l_re * h_im + l_im * h_re + b_im)
        hst_ref[:, re] = h_re
        hst_ref[:, im] = h_im

    def chunk_outputs(j):
        for q in (2 * j, 2 * j + 1):
            cols = slice(q * tile, (q + 1) * tile)
            xq = x_ref[:, cols].astype(BF16)
            h_re = st_ref[:, cols].astype(BF16)
            h_im = st_ref[:, n_state + q * tile:n_state + (q + 1) * tile].astype(BF16)
            yc_ref[:, cols] = (_dot(h_re, fre_ref[q]) + _dot(h_im, fim_ref[q])
                               + _dot(xq, tz_ref[q]))

    def scatter(j):
        for c in range(n_c):
            r0 = c * batch
            for rp in range(SSM_CHUNK // 2):
                col = (2 * j) * tile + rp * LANES
                a = _gelu_tanh(yc_ref[r0:r0 + batch, col:col + LANES])
                b = _gelu_tanh(yc_ref[r0:r0 + batch, col + tile:col + tile + LANES])
                t0 = (c * SSM_CHUNK + 2 * rp) * batch
                ytm_ref[slot_a, t0:t0 + batch, j * LANES:(j + 1) * LANES] = jnp.where(
                    lo, a, pltpu.roll(b, half, 1))
                ytm_ref[slot_a, t0 + batch:t0 + 2 * batch, j * LANES:(j + 1) * LANES] = jnp.where(
                    lo, pltpu.roll(a, half, 1), b)

    phases = (gather, chunk_inputs, recurrence, chunk_outputs, scatter)
    assert n_pairs == 4 and n_state == n_q * tile, "the schedule below is written for 4 quad pairs"
    utm_ref[...] = jnp.swapaxes(u_ref[...].astype(F32), 0, 1).reshape(n_t * batch, sw)
    for tick in range(n_pairs + len(phases) - 1):
        for depth, phase in enumerate(phases):
            j = tick - depth
            if 0 <= j < n_pairs:
                phase(j)
        if tick % 2 == 1:
            post(tick // 2)


def _s5(u, ere, eim, lcre, lcim, fre, fim, tz, glu_w, glu_b, ssm_p):
    batch, s, sw = u.shape
    d = ssm_p.shape[1]
    n_state = lcre.shape[1]
    n_t = TIME_TILE
    n_blocks = s // n_t
    rows_c = (n_t // SSM_CHUNK) * batch
    kern = functools.partial(_s5_kernel, n_state=n_state)
    consts = (ere, eim, lcre, lcim, fre, fim, tz, glu_w, glu_b, ssm_p)
    return pl.pallas_call(
        kern,
        grid=(n_blocks + 1,),
        in_specs=[pl.BlockSpec((batch, n_t, sw), lambda i: (0, jnp.minimum(i, n_blocks - 1), 0))]
                 + [_const_spec(c.shape) for c in consts],
        out_specs=pl.BlockSpec((batch, n_t, d), lambda i: (0, jnp.maximum(i - 1, 0), 0)),
        out_shape=jax.ShapeDtypeStruct((batch, s, d), BF16),
        scratch_shapes=[pltpu.VMEM((batch * n_t, sw), F32),
                        pltpu.VMEM((rows_c, SSM_CHUNK * sw), F32),
                        pltpu.VMEM((rows_c, 2 * n_state), F32),
                        pltpu.VMEM((batch, 2 * n_state), F32),
                        pltpu.VMEM((rows_c, SSM_CHUNK * sw), F32),
                        pltpu.VMEM((2, batch * n_t, sw), F32)],
        compiler_params=pltpu.CompilerParams(
            dimension_semantics=("arbitrary",),
            vmem_limit_bytes=VMEM_LIMIT_BYTES),
        name="s5",
    )(u, *consts)


def _ffn_kernel(x_ref, part_ref, sgs_ref, yssm_ref, wo_ref, g2_ref, wg_ref, wu_ref, wd_ref,
                gf_ref, o_ref, *, chunks):
    tb = x_ref.shape[0] // FFN_SUBTILES
    subs = [slice(sub * tb, (sub + 1) * tb) for sub in range(FFN_SUBTILES)]
    x1s = []
    for rows in subs:
        merged = (part_ref[rows, :].astype(F32)
                  + sgs_ref[rows, :].astype(F32) * yssm_ref[rows, :].astype(F32))
        x1s.append(x_ref[rows, :] + _dot(merged.astype(BF16), wo_ref[...]))
    h2s = [_rms(x1, g2_ref[...]).astype(BF16) for x1 in x1s]
    outs = []
    for x1, h2 in zip(x1s, h2s):
        acc = x1
        for lo, hi in chunks:
            g = _dot(h2, wg_ref[:, lo:hi])
            a = (g * jax.nn.sigmoid(g) * _dot(h2, wu_ref[:, lo:hi])).astype(BF16)
            acc = acc + _dot(a, wd_ref[lo:hi, :])
        outs.append(_rms(acc, gf_ref[...]))
    o_ref[...] = jnp.concatenate(outs, axis=0)


def _ffn_chunks(hidden):
    step = 4 * MXU_TILE
    return tuple((lo, min(lo + step, hidden)) for lo in range(0, hidden, step))


def _ffn(x, part, sgs, yssm, w_o, g2, wg, wu, wd, gf):
    b, s, d = x.shape
    tb = FFN_ROW_TILE
    kern = functools.partial(_ffn_kernel, chunks=_ffn_chunks(wg.shape[1]))
    row_spec = pl.BlockSpec((None, tb, d), lambda bi, i: (bi, i, 0))
    return pl.pallas_call(
        kern,
        grid=(b, s // tb),
        in_specs=[row_spec, row_spec, row_spec, row_spec,
                  _const_spec(w_o.shape), _const_spec((1, d)),
                  _const_spec(wg.shape), _const_spec(wu.shape), _const_spec(wd.shape),
                  _const_spec((1, d))],
        out_specs=row_spec,
        out_shape=jax.ShapeDtypeStruct((b, s, d), x.dtype),
        compiler_params=pltpu.CompilerParams(
            dimension_semantics=("arbitrary", "arbitrary"),
            vmem_limit_bytes=FFN_VMEM_LIMIT_BYTES),
        name="ffn",
    )(x, part, sgs, yssm, w_o, g2, wg, wu, wd, gf)


def _pack_toeplitz(kq, d_skip, h):
    n, n_q, qc, _ = kq.shape
    idx = jnp.arange(qc)
    same_group = (idx[:, None] // h == idx[None, :] // h).astype(F32)
    blk = kq * same_group
    blk = blk.at[0].add(jnp.eye(qc, dtype=F32) * d_skip.reshape(n_q, 1, qc))
    zero = jnp.zeros_like(blk[0])
    rows = [jnp.concatenate([blk[r - k] if r >= k else zero for r in range(n)], axis=-1)
            for k in range(n)]
    return jnp.concatenate(rows, axis=1)


def kernel(x, mem, norm1_g, w_in, conv_w, conv_proj, ssm_A_re, ssm_A_im, ssm_log_dt, ssm_B_re,
           ssm_B_im, ssm_C_re, ssm_C_im, ssm_D, ssm_glu_w, ssm_glu_b, ssm_proj, mem_norm_g,
           attn_wk, attn_wv, attn_proj, w_o, norm2_g, ffn_w_gate, ffn_w_up, ffn_w_down,
           final_norm_g):
    b, s, d = x.shape
    depth = w_in.shape[0]
    cw = conv_w.shape[2]
    sw = ssm_D.shape[1]
    mw = attn_wk.shape[2]
    n_groups = ssm_A_re.shape[1]
    n_state = n_groups * SSM_STATE
    assert depth == 1, "the final norm is fused into the single layer's ffn kernel"
    assert b == SUBLANES, "the S5 recurrence keeps the batch on the sublane axis"

    for l in range(depth):
        lc_re, lc_im, e_re, e_im, f_re, f_im, kq = _ssm_prep(
            ssm_A_re[l], ssm_A_im[l], ssm_log_dt[l],
            ssm_B_re[l].transpose(0, 2, 1), ssm_B_im[l].transpose(0, 2, 1),
            ssm_C_re[l], ssm_C_im[l])
        ere = e_re.astype(BF16)
        eim = e_im.astype(BF16)
        fre = f_re.transpose(0, 2, 1).astype(BF16)
        fim = f_im.transpose(0, 2, 1).astype(BF16)
        tz = _pack_toeplitz(kq, ssm_D[l], SSM_GROUP).astype(BF16)
        lcre = jnp.broadcast_to(lc_re.reshape(1, n_state), (b, n_state))
        lcim = jnp.broadcast_to(lc_im.reshape(1, n_state), (b, n_state))

        kt, v = _mem_kv(mem, mem_norm_g[l].reshape(1, d), attn_wk[l].T.astype(BF16),
                        attn_wv[l].astype(BF16))
        u, sgs, part = _mixer(
            x, norm1_g[l].reshape(1, d), w_in[l].astype(BF16), conv_w[l],
            conv_proj[l].astype(BF16), kt, v, attn_proj[l].astype(BF16), cw=cw, sw=sw, mw=mw)
        yssm = _s5(u, ere, eim, lcre, lcim, fre, fim, tz,
                   ssm_glu_w[l].astype(BF16), ssm_glu_b[l].reshape(1, sw),
                   ssm_proj[l].astype(BF16))
        x = _ffn(x, part, sgs, yssm, w_o[l].astype(BF16),
                 norm2_g[l].reshape(1, d), ffn_w_gate[l].astype(BF16), ffn_w_up[l].astype(BF16),
                 ffn_w_down[l].astype(BF16), final_norm_g.reshape(1, d))
    return x
```

```python
import functools
import math

import jax
import jax.numpy as jnp
from jax import lax
from jax.experimental import pallas as pl
from jax.experimental.pallas import tpu as pltpu

F32 = jnp.float32
BF16 = jnp.bfloat16

EPS = 1e-6
SSM_GROUP = 16
SSM_STATE = 64
MEM_HEADS = 4
MEM_HEAD_DIM = 128

MXU_TILE = 256
SUBLANES = 8
LANES = 128
VMEM_LIMIT_BYTES = 56 * 1024 * 1024
FFN_VMEM_LIMIT_BYTES = 62 * 1024 * 1024

MIXER_ROW_TILE = 1024
FFN_ROW_TILE = 1024
FFN_SUBTILES = 4
TIME_TILE = 256
SSM_CHUNK = 4
QUAD_CH = MXU_TILE // SSM_CHUNK
POST_ROWS = 512


def _rms(x, g):
    ms = jnp.mean(x * x, axis=-1, keepdims=True)
    return x * lax.rsqrt(ms + EPS) * g


def _dot(a, b):
    return jnp.dot(a, b, preferred_element_type=F32)


def _gelu_tanh(x):
    k0 = -2.0 * math.sqrt(2.0 / math.pi) * math.log2(math.e)
    w = x * (k0 + (k0 * 0.044715) * (x * x))
    return x / (1.0 + jnp.exp2(w))


def _sigmoid(x):
    return 0.5 + 0.5 * jnp.tanh(0.5 * x)


def _const_spec(shape):
    nd = len(shape)
    return pl.BlockSpec(shape, lambda *_: (0,) * nd, pipeline_mode=pl.Buffered(1))


def _ssm_prep_kernel(are_ref, aim_ref, ldt_ref, bre_ref, bim_ref, cre_ref, cim_ref,
                     lcre_ref, lcim_ref, ere_ref, eim_ref, fre_ref, fim_ref, kq_ref):
    a_re = are_ref[...]
    a_im = aim_ref[...]
    dt = jnp.exp(ldt_ref[...])
    mag = jnp.exp(a_re * dt)
    l_re = mag * jnp.cos(a_im * dt)
    l_im = mag * jnp.sin(a_im * dt)
    n_re = l_re - 1.0
    n_im = l_im
    den = a_re * a_re + a_im * a_im
    q_re = ((n_re * a_re + n_im * a_im) / den)[:, None, :]
    q_im = ((n_im * a_re - n_re * a_im) / den)[:, None, :]
    b_re = bre_ref[...]
    b_im = bim_ref[...]
    bb_re = q_re * b_re - q_im * b_im
    bb_im = q_re * b_im + q_im * b_re
    c_re = cre_ref[...]
    c_im = cim_ref[...]
    g, h, p = c_re.shape

    pows = [(jnp.ones_like(l_re), jnp.zeros_like(l_re))]
    for _ in range(SSM_CHUNK):
        pr, pi = pows[-1]
        pows.append((pr * l_re - pi * l_im, pr * l_im + pi * l_re))
    lcre_ref[...] = pows[SSM_CHUNK][0]
    lcim_ref[...] = pows[SSM_CHUNK][1]

    def place(out_ref, blocks):
        per_quad = QUAD_CH // h
        out_ref[...] = jnp.zeros(out_ref.shape, F32)
        for k, blk in enumerate(blocks):
            for gi in range(g):
                a = gi % per_quad
                r0 = (k * per_quad + a) * h
                out_ref[gi // per_quad, r0:r0 + h, a * p:(a + 1) * p] = blk[gi]

    e_re, e_im, f_re, f_im = [], [], [], []
    for k in range(SSM_CHUNK):
        pr, pi = pows[SSM_CHUNK - 1 - k]
        pr, pi = pr[:, None, :], pi[:, None, :]
        e_re.append(pr * bb_re - pi * bb_im)
        e_im.append(pr * bb_im + pi * bb_re)
    for r in range(SSM_CHUNK):
        pr, pi = pows[r + 1]
        pr, pi = pr[:, None, :], pi[:, None, :]
        f_re.append(c_re * pr - c_im * pi)
        f_im.append(-(c_re * pi + c_im * pr))
    place(ere_ref, e_re)
    place(eim_ref, e_im)
    place(fre_ref, f_re)
    place(fim_ref, f_im)
    nt = (((1,), (1,)), ((), ()))
    c_re2 = c_re.reshape(g * h, p)
    c_im2 = c_im.reshape(g * h, p)
    for j in range(SSM_CHUNK):
        pr, pi = pows[j]
        pr, pi = pr[:, None, :], pi[:, None, :]
        w_re = (pr * bb_re - pi * bb_im).reshape(g * h, p)
        w_im = (pr * bb_im + pi * bb_re).reshape(g * h, p)
        kk = (lax.dot_general(w_re, c_re2, nt, precision=lax.Precision.HIGHEST,
                              preferred_element_type=F32)
              - lax.dot_general(w_im, c_im2, nt, precision=lax.Precision.HIGHEST,
                                preferred_element_type=F32))
        for q in range(g * h // QUAD_CH):
            kq_ref[j, q] = kk[q * QUAD_CH:(q + 1) * QUAD_CH, q * QUAD_CH:(q + 1) * QUAD_CH]


def _ssm_prep(a_re, a_im, log_dt, b_re_t, b_im_t, c_re, c_im):
    g, p = a_re.shape
    h = b_re_t.shape[1]
    n = SSM_CHUNK
    return pl.pallas_call(
        _ssm_prep_kernel,
        out_shape=(jax.ShapeDtypeStruct((g, p), F32), jax.ShapeDtypeStruct((g, p), F32),
                   *[jax.ShapeDtypeStruct((g * h // QUAD_CH, n * QUAD_CH, n * QUAD_CH), F32)] * 4,
                   jax.ShapeDtypeStruct((n, g * h // QUAD_CH, QUAD_CH, QUAD_CH), F32)),
        name="ssm_prep",
    )(a_re, a_im, log_dt.reshape(g, 1), b_re_t, b_im_t, c_re, c_im)


def _mem_kv_kernel(mem_ref, g_ref, wkt_ref, wv_ref, kt_ref, v_ref):
    mn = _rms(mem_ref[...], g_ref[...]).astype(BF16)
    kt = lax.dot_general(wkt_ref[...], mn, (((1,), (1,)), ((), ())),
                         preferred_element_type=F32)
    kt_ref[...] = kt.astype(BF16)
    v_ref[...] = _dot(mn, wv_ref[...]).astype(BF16)


def _mem_kv(mem, g, wkt, wv):
    b, m, d = mem.shape
    w = wv.shape[1]
    return pl.pallas_call(
        _mem_kv_kernel,
        grid=(b,),
        in_specs=[pl.BlockSpec((None, m, d), lambda i: (i, 0, 0)),
                  pl.BlockSpec((1, d), lambda i: (0, 0)),
                  pl.BlockSpec((w, d), lambda i: (0, 0)),
                  pl.BlockSpec((d, w), lambda i: (0, 0))],
        out_specs=(pl.BlockSpec((None, w, m), lambda i: (i, 0, 0)),
                   pl.BlockSpec((None, m, w), lambda i: (i, 0, 0))),
        out_shape=(jax.ShapeDtypeStruct((b, w, m), BF16), jax.ShapeDtypeStruct((b, m, w), BF16)),
        name="mem_kv",
    )(mem, g, wkt, wv)


def _mixer_kernel(x_ref, g1_ref, win_ref, convw_ref, convp_ref, kt_ref, v_ref, attnp_ref,
                  u_ref, sgs_ref, part_ref, cv_ref, *, cw, sw, mw, d):
    i = pl.program_id(1)
    tb = x_ref.shape[0]
    h = _rms(x_ref[...], g1_ref[...]).astype(BF16)

    def proj(lo, hi):
        return _dot(h, win_ref[:, lo:hi])

    o_u = 3 * cw
    o_q = o_u + sw
    o_gc = o_q + mw
    o_gs = o_gc + d
    o_gm = o_gs + d

    q = proj(o_q, o_gc)
    scale = MEM_HEAD_DIM ** -0.5
    scores = []
    for hd in range(MEM_HEADS):
        lo = hd * MEM_HEAD_DIM
        scores.append(_dot(q[:, lo:lo + MEM_HEAD_DIM].astype(BF16),
                           kt_ref[lo:lo + MEM_HEAD_DIM, :]) * scale)
    c_gate = proj(cw, 2 * cw)
    v_conv = proj(2 * cw, 3 * cw)

    heads = []
    for hd in range(MEM_HEADS):
        lo = hd * MEM_HEAD_DIM
        s = scores[hd]
        e = jnp.exp(s - jnp.max(s, axis=-1, keepdims=True))
        pv = _dot(e.astype(BF16), v_ref[:, lo:lo + MEM_HEAD_DIM])
        heads.append(pv / jnp.sum(e, axis=-1, keepdims=True))
    o = jnp.concatenate(heads, axis=-1).astype(BF16)
    b_gate = proj(0, cw)

    cv = c_gate * v_conv

    @pl.when(i == 0)
    def _():
        cv_ref[0:SUBLANES, :] = jnp.zeros((SUBLANES, cw), F32)

    cv_ref[SUBLANES:SUBLANES + tb, :] = cv
    conv = (convw_ref[0:1, :] * cv_ref[SUBLANES - 2:SUBLANES - 2 + tb, :]
            + convw_ref[1:2, :] * cv_ref[SUBLANES - 1:SUBLANES - 1 + tb, :]
            + convw_ref[2:3, :] * cv)
    cv_ref[0:SUBLANES, :] = cv_ref[tb:tb + SUBLANES, :]
    pre = (b_gate * conv).astype(BF16)

    g_c = proj(o_gc, o_gs)
    y_mem = _dot(o, attnp_ref[...])
    y_conv = _dot(pre, convp_ref[...])
    g_m = proj(o_gm, o_gm + d)
    part_ref[...] = (_sigmoid(g_c) * y_conv + _sigmoid(g_m) * y_mem).astype(BF16)
    sgs_ref[...] = _sigmoid(proj(o_gs, o_gm)).astype(BF16)
    u_ref[...] = proj(o_u, o_q).astype(BF16)


def _mixer(x, g1, w_in, conv_w, conv_p, kt, v, attn_p, *, cw, sw, mw):
    b, s, d = x.shape
    tb = MIXER_ROW_TILE
    m = kt.shape[2]
    kern = functools.partial(_mixer_kernel, cw=cw, sw=sw, mw=mw, d=d)
    return pl.pallas_call(
        kern,
        grid=(b, s // tb),
        in_specs=[pl.BlockSpec((None, tb, d), lambda bi, i: (bi, i, 0)),
                  _const_spec((1, d)),
                  _const_spec(w_in.shape),
                  _const_spec(conv_w.shape),
                  _const_spec(conv_p.shape),
                  pl.BlockSpec((None, mw, m), lambda bi, i: (bi, 0, 0)),
                  pl.BlockSpec((None, m, mw), lambda bi, i: (bi, 0, 0)),
                  _const_spec(attn_p.shape)],
        out_specs=(pl.BlockSpec((None, tb, sw), lambda bi, i: (bi, i, 0)),
                   pl.BlockSpec((None, tb, d), lambda bi, i: (bi, i, 0)),
                   pl.BlockSpec((None, tb, d), lambda bi, i: (bi, i, 0))),
        out_shape=(jax.ShapeDtypeStruct((b, s, sw), BF16),
                   jax.ShapeDtypeStruct((b, s, d), BF16),
                   jax.ShapeDtypeStruct((b, s, d), BF16)),
        scratch_shapes=[pltpu.VMEM((tb + SUBLANES, cw), F32)],
        compiler_params=pltpu.CompilerParams(
            dimension_semantics=("arbitrary", "arbitrary"),
            vmem_limit_bytes=VMEM_LIMIT_BYTES),
        name="mixer",
    )(x, g1, w_in, conv_w, conv_p, kt, v, attn_p)


def _s5_kernel(u_ref, ere_ref, eim_ref, lcre_ref, lcim_ref, fre_ref, fim_ref, tz_ref,
               gw_ref, gb_ref, sp_ref, y_ref,
               utm_ref, x_ref, st_ref, hst_ref, yc_ref, ytm_ref, *, n_state):
    batch, n_t, sw = u_ref.shape
    n_c = n_t // SSM_CHUNK
    n_q = sw // QUAD_CH
    tile = SSM_CHUNK * QUAD_CH
    half = LANES // 2
    lo = lax.broadcasted_iota(jnp.int32, (batch, LANES), 1) < half
    step_i = pl.program_id(0)
    slot_a = step_i % 2
    slot_b = 1 - slot_a

    @pl.when(step_i == 0)
    def _():
        hst_ref[...] = jnp.zeros_like(hst_ref)
        ytm_ref[1] = jnp.zeros(ytm_ref.shape[1:], F32)

    sub_t = POST_ROWS // batch

    def post(sb):
        y = ytm_ref[slot_b, sb * POST_ROWS:(sb + 1) * POST_ROWS, :].reshape(sub_t, batch, sw)
        y = jnp.swapaxes(y, 0, 1).reshape(POST_ROWS, sw)
        half_y = 0.5 * y
        y = half_y + half_y * jnp.tanh(0.5 * (_dot(y.astype(BF16), gw_ref[...]) + gb_ref[...]))
        out = _dot(y.astype(BF16), sp_ref[...]).astype(BF16)
        y_ref[:, sb * sub_t:(sb + 1) * sub_t, :] = out.reshape(batch, sub_t, -1)

    assert n_t // sub_t == 4, "stage B is interleaved as four sub-blocks"

    n_pairs = n_q // 2

    def gather(j):
        for c in range(n_c):
            r0 = c * batch
            for kp in range(SSM_CHUNK // 2):
                t0 = (c * SSM_CHUNK + 2 * kp) * batch
                s0 = utm_ref[t0:t0 + batch, j * LANES:(j + 1) * LANES]
                s1 = utm_ref[t0 + batch:t0 + 2 * batch, j * LANES:(j + 1) * LANES]
                col = (2 * j) * tile + kp * LANES
                x_ref[r0:r0 + batch, col:col + LANES] = jnp.where(lo, s0, pltpu.roll(s1, half, 1))
                col += tile
                x_ref[r0:r0 + batch, col:col + LANES] = jnp.where(lo, pltpu.roll(s0, half, 1), s1)

    def chunk_inputs(j):
        for q in (2 * j, 2 * j + 1):
            xq = x_ref[:, q * tile:(q + 1) * tile].astype(BF16)
            st_ref[:, q * tile:(q + 1) * tile] = _dot(xq, ere_ref[q])
            st_ref[:, n_state + q * tile:n_state + (q + 1) * tile] = _dot(xq, eim_ref[q])

    def recurrence(j):
        re = slice(j * 2 * tile, (j + 1) * 2 * tile)
        im = slice(n_state + j * 2 * tile, n_state + (j + 1) * 2 * tile)
        l_re = lcre_ref[:, re]
        l_im = lcim_ref[:, re]
        h_re = hst_ref[:, re]
        h_im = hst_ref[:, im]
        for c in range(n_c):
            rows = slice(c * batch, (c + 1) * batch)
            b_re = st_ref[rows, re]
            b_im = st_ref[rows, im]
            st_ref[rows, re] = h_re
            st_ref[rows, im] = h_im
            h_re, h_im = (l_re * h_re - l_im * h_im + b_re, l_re * h_im + l_im * h_re + b_im)
        hst_ref[:, re] = h_re
        hst_ref[:, im] = h_im

    def chunk_outputs(j):
        for q in (2 * j, 2 * j + 1):
            cols = slice(q * tile, (q + 1) * tile)
            xq = x_ref[:, cols].astype(BF16)
            h_re = st_ref[:, cols].astype(BF16)
            h_im = st_ref[:, n_state + q * tile:n_state + (q + 1) * tile].astype(BF16)
            yc_ref[:, cols] = (_dot(h_re, fre_ref[q]) + _dot(h_im, fim_ref[q])
                               + _dot(xq, tz_ref[q]))

    def scatter(j):
        for c in range(n_c):
            r0 = c * batch
            for rp in range(SSM_CHUNK // 2):
                col = (2 * j) * tile + rp * LANES
                a = _gelu_tanh(yc_ref[r0:r0 + batch, col:col + LANES])
                b = _gelu_tanh(yc_ref[r0:r0 + batch, col + tile:col + tile + LANES])
                t0 = (c * SSM_CHUNK + 2 * rp) * batch
                ytm_ref[slot_a, t0:t0 + batch, j * LANES:(j + 1) * LANES] = jnp.where(
                    lo, a, pltpu.roll(b, half, 1))
                ytm_ref[slot_a, t0 + batch:t0 + 2 * batch, j * LANES:(j + 1) * LANES] = jnp.where(
                    lo, pltpu.roll(a, half, 1), b)

    phases = (gather, chunk_inputs, recurrence, chunk_outputs, scatter)
    assert n_pairs == 4 and n_state == n_q * tile, "the schedule below is written for 4 quad pairs"
    utm_ref[...] = jnp.swapaxes(u_ref[...].astype(F32), 0, 1).reshape(n_t * batch, sw)
    for tick in range(n_pairs + len(phases) - 1):
        for depth, phase in enumerate(phases):
            j = tick - depth
            if 0 <= j < n_pairs:
                phase(j)
        if tick % 2 == 1:
            post(tick // 2)


def _s5(u, ere, eim, lcre, lcim, fre, fim, tz, glu_w, glu_b, ssm_p):
    batch, s, sw = u.shape
    d = ssm_p.shape[1]
    n_state = lcre.shape[1]
    n_t = TIME_TILE
    n_blocks = s // n_t
    rows_c = (n_t // SSM_CHUNK) * batch
    kern = functools.partial(_s5_kernel, n_state=n_state)
    consts = (ere, eim, lcre, lcim, fre, fim, tz, glu_w, glu_b, ssm_p)
    return pl.pallas_call(
        kern,
        grid=(n_blocks + 1,),
        in_specs=[pl.BlockSpec((batch, n_t, sw), lambda i: (0, jnp.minimum(i, n_blocks - 1), 0))]
                 + [_const_spec(c.shape) for c in consts],
        out_specs=pl.BlockSpec((batch, n_t, d), lambda i: (0, jnp.maximum(i - 1, 0), 0)),
        out_shape=jax.ShapeDtypeStruct((batch, s, d), BF16),
        scratch_shapes=[pltpu.VMEM((batch * n_t, sw), F32),
                        pltpu.VMEM((rows_c, SSM_CHUNK * sw), F32),
                        pltpu.VMEM((rows_c, 2 * n_state), F32),
                        pltpu.VMEM((batch, 2 * n_state), F32),
                        pltpu.VMEM((rows_c, SSM_CHUNK * sw), F32),
                        pltpu.VMEM((2, batch * n_t, sw), F32)],
        compiler_params=pltpu.CompilerParams(
            dimension_semantics=("arbitrary",),
            vmem_limit_bytes=VMEM_LIMIT_BYTES),
        name="s5",
    )(u, *consts)


def _ffn_kernel(x_ref, part_ref, sgs_ref, yssm_ref, wo_ref, g2_ref, wg_ref, wu_ref, wd_ref,
                gf_ref, o_ref, *, chunks):
    tb = x_ref.shape[0] // FFN_SUBTILES
    subs = [slice(sub * tb, (sub + 1) * tb) for sub in range(FFN_SUBTILES)]
    x1s = []
    for rows in subs:
        merged = (part_ref[rows, :].astype(F32)
                  + sgs_ref[rows, :].astype(F32) * yssm_ref[rows, :].astype(F32))
        x1s.append(x_ref[rows, :] + _dot(merged.astype(BF16), wo_ref[...]))
    h2s = [_rms(x1, g2_ref[...]).astype(BF16) for x1 in x1s]
    outs = []
    for x1, h2 in zip(x1s, h2s):
        acc = x1
        for lo, hi in chunks:
            g = _dot(h2, wg_ref[:, lo:hi])
            a = (g * _sigmoid(g) * _dot(h2, wu_ref[:, lo:hi])).astype(BF16)
            acc = acc + _dot(a, wd_ref[lo:hi, :])
        outs.append(_rms(acc, gf_ref[...]))
    o_ref[...] = jnp.concatenate(outs, axis=0)


def _ffn_chunks(hidden):
    step = 4 * MXU_TILE
    return tuple((lo, min(lo + step, hidden)) for lo in range(0, hidden, step))


def _ffn(x, part, sgs, yssm, w_o, g2, wg, wu, wd, gf):
    b, s, d = x.shape
    tb = FFN_ROW_TILE
    kern = functools.partial(_ffn_kernel, chunks=_ffn_chunks(wg.shape[1]))
    row_spec = pl.BlockSpec((None, tb, d), lambda bi, i: (bi, i, 0))
    return pl.pallas_call(
        kern,
        grid=(b, s // tb),
        in_specs=[row_spec, row_spec, row_spec, row_spec,
                  _const_spec(w_o.shape), _const_spec((1, d)),
                  _const_spec(wg.shape), _const_spec(wu.shape), _const_spec(wd.shape),
                  _const_spec((1, d))],
        out_specs=row_spec,
        out_shape=jax.ShapeDtypeStruct((b, s, d), x.dtype),
        compiler_params=pltpu.CompilerParams(
            dimension_semantics=("arbitrary", "arbitrary"),
            vmem_limit_bytes=FFN_VMEM_LIMIT_BYTES),
        name="ffn",
    )(x, part, sgs, yssm, w_o, g2, wg, wu, wd, gf)


def _pack_toeplitz(kq, d_skip, h):
    n, n_q, qc, _ = kq.shape
    idx = jnp.arange(qc)
    same_group = (idx[:, None] // h == idx[None, :] // h).astype(F32)
    blk = kq * same_group
    blk = blk.at[0].add(jnp.eye(qc, dtype=F32) * d_skip.reshape(n_q, 1, qc))
    zero = jnp.zeros_like(blk[0])
    rows = [jnp.concatenate([blk[r - k] if r >= k else zero for r in range(n)], axis=-1)
            for k in range(n)]
    return jnp.concatenate(rows, axis=1)


def kernel(x, mem, norm1_g, w_in, conv_w, conv_proj, ssm_A_re, ssm_A_im, ssm_log_dt, ssm_B_re,
           ssm_B_im, ssm_C_re, ssm_C_im, ssm_D, ssm_glu_w, ssm_glu_b, ssm_proj, mem_norm_g,
           attn_wk, attn_wv, attn_proj, w_o, norm2_g, ffn_w_gate, ffn_w_up, ffn_w_down,
           final_norm_g):
    b, s, d = x.shape
    depth = w_in.shape[0]
    cw = conv_w.shape[2]
    sw = ssm_D.shape[1]
    mw = attn_wk.shape[2]
    n_groups = ssm_A_re.shape[1]
    n_state = n_groups * SSM_STATE
    assert depth == 1, "the final norm is fused into the single layer's ffn kernel"
    assert b == SUBLANES, "the S5 recurrence keeps the batch on the sublane axis"

    for l in range(depth):
        lc_re, lc_im, e_re, e_im, f_re, f_im, kq = _ssm_prep(
            ssm_A_re[l], ssm_A_im[l], ssm_log_dt[l],
            ssm_B_re[l].transpose(0, 2, 1), ssm_B_im[l].transpose(0, 2, 1),
            ssm_C_re[l], ssm_C_im[l])
        ere = e_re.astype(BF16)
        eim = e_im.astype(BF16)
        fre = f_re.transpose(0, 2, 1).astype(BF16)
        fim = f_im.transpose(0, 2, 1).astype(BF16)
        tz = _pack_toeplitz(kq, ssm_D[l], SSM_GROUP).astype(BF16)
        lcre = jnp.broadcast_to(lc_re.reshape(1, n_state), (b, n_state))
        lcim = jnp.broadcast_to(lc_im.reshape(1, n_state), (b, n_state))

        kt, v = _mem_kv(mem, mem_norm_g[l].reshape(1, d), attn_wk[l].T.astype(BF16),
                        attn_wv[l].astype(BF16))
        u, sgs, part = _mixer(
            x, norm1_g[l].reshape(1, d), w_in[l].astype(BF16), conv_w[l],
            conv_proj[l].astype(BF16), kt, v, attn_proj[l].astype(BF16), cw=cw, sw=sw, mw=mw)
        yssm = _s5(u, ere, eim, lcre, lcim, fre, fim, tz,
                   ssm_glu_w[l].astype(BF16), ssm_glu_b[l].reshape(1, sw),
                   ssm_proj[l].astype(BF16))
        x = _ffn(x, part, sgs, yssm, w_o[l].astype(BF16),
                 norm2_g[l].reshape(1, d), ffn_w_gate[l].astype(BF16), ffn_w_up[l].astype(BF16),
                 ffn_w_down[l].astype(BF16), final_norm_g.reshape(1, d))
    return x
```

```python
import functools
import math

import jax
import jax.numpy as jnp
from jax import lax
from jax.experimental import pallas as pl
from jax.experimental.pallas import tpu as pltpu

F32 = jnp.float32
BF16 = jnp.bfloat16

EPS = 1e-6
SSM_GROUP = 16
SSM_STATE = 64
MEM_HEADS = 4
MEM_HEAD_DIM = 128

MXU_TILE = 256
SUBLANES = 8
LANES = 128
VMEM_LIMIT_BYTES = 56 * 1024 * 1024
FFN_VMEM_LIMIT_BYTES = 62 * 1024 * 1024

MIXER_ROW_TILE = 1024
FFN_ROW_TILE = 1024
FFN_SUBTILES = 4
TIME_TILE = 256
SSM_CHUNK = 4
QUAD_CH = MXU_TILE // SSM_CHUNK
POST_ROWS = 512


def _rms(x, g):
    ms = jnp.mean(x * x, axis=-1, keepdims=True)
    return x * lax.rsqrt(ms + EPS) * g


def _dot(a, b):
    return jnp.dot(a, b, preferred_element_type=F32)


def _gelu_tanh(x):
    k0 = -2.0 * math.sqrt(2.0 / math.pi) * math.log2(math.e)
    w = x * (k0 + (k0 * 0.044715) * (x * x))
    return x / (1.0 + jnp.exp2(w))


def _sigmoid(x):
    return 0.5 + 0.5 * jnp.tanh(0.5 * x)


def _const_spec(shape):
    nd = len(shape)
    return pl.BlockSpec(shape, lambda *_: (0,) * nd, pipeline_mode=pl.Buffered(1))


def _ssm_prep_kernel(are_ref, aim_ref, ldt_ref, bre_ref, bim_ref, cre_ref, cim_ref,
                     lcre_ref, lcim_ref, ere_ref, eim_ref, fre_ref, fim_ref, kq_ref):
    a_re = are_ref[...]
    a_im = aim_ref[...]
    dt = jnp.exp(ldt_ref[...])
    mag = jnp.exp(a_re * dt)
    l_re = mag * jnp.cos(a_im * dt)
    l_im = mag * jnp.sin(a_im * dt)
    n_re = l_re - 1.0
    n_im = l_im
    den = a_re * a_re + a_im * a_im
    q_re = ((n_re * a_re + n_im * a_im) / den)[:, None, :]
    q_im = ((n_im * a_re - n_re * a_im) / den)[:, None, :]
    b_re = bre_ref[...]
    b_im = bim_ref[...]
    bb_re = q_re * b_re - q_im * b_im
    bb_im = q_re * b_im + q_im * b_re
    c_re = cre_ref[...]
    c_im = cim_ref[...]
    g, h, p = c_re.shape

    pows = [(jnp.ones_like(l_re), jnp.zeros_like(l_re))]
    for _ in range(SSM_CHUNK):
        pr, pi = pows[-1]
        pows.append((pr * l_re - pi * l_im, pr * l_im + pi * l_re))
    lcre_ref[...] = pows[SSM_CHUNK][0]
    lcim_ref[...] = pows[SSM_CHUNK][1]

    def place(out_ref, blocks):
        per_quad = QUAD_CH // h
        out_ref[...] = jnp.zeros(out_ref.shape, F32)
        for k, blk in enumerate(blocks):
            for gi in range(g):
                a = gi % per_quad
                r0 = (k * per_quad + a) * h
                out_ref[gi // per_quad, r0:r0 + h, a * p:(a + 1) * p] = blk[gi]

    e_re, e_im, f_re, f_im = [], [], [], []
    for k in range(SSM_CHUNK):
        pr, pi = pows[SSM_CHUNK - 1 - k]
        pr, pi = pr[:, None, :], pi[:, None, :]
        e_re.append(pr * bb_re - pi * bb_im)
        e_im.append(pr * bb_im + pi * bb_re)
    for r in range(SSM_CHUNK):
        pr, pi = pows[r + 1]
        pr, pi = pr[:, None, :], pi[:, None, :]
        f_re.append(c_re * pr - c_im * pi)
        f_im.append(-(c_re * pi + c_im * pr))
    place(ere_ref, e_re)
    place(eim_ref, e_im)
    place(fre_ref, f_re)
    place(fim_ref, f_im)
    nt = (((1,), (1,)), ((), ()))
    c_re2 = c_re.reshape(g * h, p)
    c_im2 = c_im.reshape(g * h, p)
    for j in range(SSM_CHUNK):
        pr, pi = pows[j]
        pr, pi = pr[:, None, :], pi[:, None, :]
        w_re = (pr * bb_re - pi * bb_im).reshape(g * h, p)
        w_im = (pr * bb_im + pi * bb_re).reshape(g * h, p)
        kk = (lax.dot_general(w_re, c_re2, nt, precision=lax.Precision.HIGHEST,
                              preferred_element_type=F32)
              - lax.dot_general(w_im, c_im2, nt, precision=lax.Precision.HIGHEST,
                                preferred_element_type=F32))
        for q in range(g * h // QUAD_CH):
            kq_ref[j, q] = kk[q * QUAD_CH:(q + 1) * QUAD_CH, q * QUAD_CH:(q + 1) * QUAD_CH]


def _ssm_prep(a_re, a_im, log_dt, b_re_t, b_im_t, c_re, c_im):
    g, p = a_re.shape
    h = b_re_t.shape[1]
    n = SSM_CHUNK
    return pl.pallas_call(
        _ssm_prep_kernel,
        out_shape=(jax.ShapeDtypeStruct((g, p), F32), jax.ShapeDtypeStruct((g, p), F32),
                   *[jax.ShapeDtypeStruct((g * h // QUAD_CH, n * QUAD_CH, n * QUAD_CH), F32)] * 4,
                   jax.ShapeDtypeStruct((n, g * h // QUAD_CH, QUAD_CH, QUAD_CH), F32)),
        name="ssm_prep",
    )(a_re, a_im, log_dt.reshape(g, 1), b_re_t, b_im_t, c_re, c_im)


def _mem_kv_kernel(mem_ref, g_ref, wkt_ref, wv_ref, kt_ref, v_ref):
    mn = _rms(mem_ref[...], g_ref[...]).astype(BF16)
    kt = lax.dot_general(wkt_ref[...], mn, (((1,), (1,)), ((), ())),
                         preferred_element_type=F32)
    kt_ref[...] = (kt * MEM_HEAD_DIM ** -0.5).astype(BF16)
    v = _dot(mn, wv_ref[...]).astype(BF16)
    ones = jnp.ones((v.shape[0], MEM_HEAD_DIM), BF16)
    v_ref[...] = jnp.concatenate(
        [blk for hd in range(MEM_HEADS)
         for blk in (v[:, hd * MEM_HEAD_DIM:(hd + 1) * MEM_HEAD_DIM], ones)], axis=1)


def _mem_kv(mem, g, wkt, wv):
    b, m, d = mem.shape
    w = wv.shape[1]
    return pl.pallas_call(
        _mem_kv_kernel,
        grid=(b,),
        in_specs=[pl.BlockSpec((None, m, d), lambda i: (i, 0, 0)),
                  pl.BlockSpec((1, d), lambda i: (0, 0)),
                  pl.BlockSpec((w, d), lambda i: (0, 0)),
                  pl.BlockSpec((d, w), lambda i: (0, 0))],
        out_specs=(pl.BlockSpec((None, w, m), lambda i: (i, 0, 0)),
                   pl.BlockSpec((None, m, 2 * w), lambda i: (i, 0, 0))),
        out_shape=(jax.ShapeDtypeStruct((b, w, m), BF16),
                   jax.ShapeDtypeStruct((b, m, 2 * w), BF16)),
        name="mem_kv",
    )(mem, g, wkt, wv)


def _mixer_kernel(x_ref, g1_ref, win_ref, convw_ref, convp_ref, kt_ref, v_ref, attnp_ref,
                  u_ref, sgs_ref, part_ref, cv_ref, *, cw, sw, mw, d):
    i = pl.program_id(1)
    tb = x_ref.shape[0]
    h = _rms(x_ref[...], g1_ref[...]).astype(BF16)

    def proj(lo, hi):
        return _dot(h, win_ref[:, lo:hi])

    o_u = 3 * cw
    o_q = o_u + sw
    o_gc = o_q + mw
    o_gs = o_gc + d
    o_gm = o_gs + d

    q = proj(o_q, o_gc)
    scores = []
    for hd in range(MEM_HEADS):
        lo = hd * MEM_HEAD_DIM
        scores.append(_dot(q[:, lo:lo + MEM_HEAD_DIM].astype(BF16),
                           kt_ref[lo:lo + MEM_HEAD_DIM, :]))
    c_gate = proj(cw, 2 * cw)
    v_conv = proj(2 * cw, 3 * cw)

    heads = []
    for hd in range(MEM_HEADS):
        lo = hd * MEM_HEAD_DIM
        s = scores[hd]
        e = jnp.exp(s - jnp.max(s, axis=-1, keepdims=True))
        pv = _dot(e.astype(BF16), v_ref[:, 2 * lo:2 * lo + 2 * MEM_HEAD_DIM])
        heads.append(pv[:, :MEM_HEAD_DIM] / pv[:, MEM_HEAD_DIM:])
    o = jnp.concatenate(heads, axis=-1).astype(BF16)
    b_gate = proj(0, cw)

    cv = c_gate * v_conv

    @pl.when(i == 0)
    def _():
        cv_ref[0:SUBLANES, :] = jnp.zeros((SUBLANES, cw), F32)

    cv_ref[SUBLANES:SUBLANES + tb, :] = cv
    conv = (convw_ref[0:1, :] * cv_ref[SUBLANES - 2:SUBLANES - 2 + tb, :]
            + convw_ref[1:2, :] * cv_ref[SUBLANES - 1:SUBLANES - 1 + tb, :]
            + convw_ref[2:3, :] * cv)
    cv_ref[0:SUBLANES, :] = cv_ref[tb:tb + SUBLANES, :]
    pre = (b_gate * conv).astype(BF16)

    g_c = proj(o_gc, o_gs)
    y_mem = _dot(o, attnp_ref[...])
    y_conv = _dot(pre, convp_ref[...])
    g_m = proj(o_gm, o_gm + d)
    part_ref[...] = (_sigmoid(g_c) * y_conv + _sigmoid(g_m) * y_mem).astype(BF16)
    sgs_ref[...] = _sigmoid(proj(o_gs, o_gm)).astype(BF16)
    u_ref[...] = proj(o_u, o_q).astype(BF16)


def _mixer(x, g1, w_in, conv_w, conv_p, kt, v, attn_p, *, cw, sw, mw):
    b, s, d = x.shape
    tb = MIXER_ROW_TILE
    m = kt.shape[2]
    kern = functools.partial(_mixer_kernel, cw=cw, sw=sw, mw=mw, d=d)
    return pl.pallas_call(
        kern,
        grid=(b, s // tb),
        in_specs=[pl.BlockSpec((None, tb, d), lambda bi, i: (bi, i, 0)),
                  _const_spec((1, d)),
                  _const_spec(w_in.shape),
                  _const_spec(conv_w.shape),
                  _const_spec(conv_p.shape),
                  pl.BlockSpec((None, mw, m), lambda bi, i: (bi, 0, 0)),
                  pl.BlockSpec((None, m, 2 * mw), lambda bi, i: (bi, 0, 0)),
                  _const_spec(attn_p.shape)],
        out_specs=(pl.BlockSpec((None, tb, sw), lambda bi, i: (bi, i, 0)),
                   pl.BlockSpec((None, tb, d), lambda bi, i: (bi, i, 0)),
                   pl.BlockSpec((None, tb, d), lambda bi, i: (bi, i, 0))),
        out_shape=(jax.ShapeDtypeStruct((b, s, sw), BF16),
                   jax.ShapeDtypeStruct((b, s, d), BF16),
                   jax.ShapeDtypeStruct((b, s, d), BF16)),
        scratch_shapes=[pltpu.VMEM((tb + SUBLANES, cw), F32)],
        compiler_params=pltpu.CompilerParams(
            dimension_semantics=("arbitrary", "arbitrary"),
            vmem_limit_bytes=VMEM_LIMIT_BYTES),
        name="mixer",
    )(x, g1, w_in, conv_w, conv_p, kt, v, attn_p)


def _s5_kernel(u_ref, ere_ref, eim_ref, lcre_ref, lcim_ref, fre_ref, fim_ref, tz_ref,
               gw_ref, gb_ref, sp_ref, y_ref,
               utm_ref, x_ref, st_ref, hst_ref, yc_ref, ytm_ref, *, n_state):
    batch, n_t, sw = u_ref.shape
    n_c = n_t // SSM_CHUNK
    n_q = sw // QUAD_CH
    tile = SSM_CHUNK * QUAD_CH
    half = LANES // 2
    lo = lax.broadcasted_iota(jnp.int32, (batch, LANES), 1) < half
    step_i = pl.program_id(0)
    slot_a = step_i % 2
    slot_b = 1 - slot_a

    @pl.when(step_i == 0)
    def _():
        hst_ref[...] = jnp.zeros_like(hst_ref)
        ytm_ref[1] = jnp.zeros(ytm_ref.shape[1:], F32)

    sub_t = POST_ROWS // batch

    def post(sb):
        y = ytm_ref[slot_b, sb * POST_ROWS:(sb + 1) * POST_ROWS, :].reshape(sub_t, batch, sw)
        y = jnp.swapaxes(y, 0, 1).reshape(POST_ROWS, sw)
        half_y = 0.5 * y
        y = half_y + half_y * jnp.tanh(0.5 * (_dot(y.astype(BF16), gw_ref[...]) + gb_ref[...]))
        out = _dot(y.astype(BF16), sp_ref[...]).astype(BF16)
        y_ref[:, sb * sub_t:(sb + 1) * sub_t, :] = out.reshape(batch, sub_t, -1)

    assert n_t // sub_t == 4, "stage B is interleaved as four sub-blocks"

    n_pairs = n_q // 2

    def gather(j):
        for c in range(n_c):
            r0 = c * batch
            for kp in range(SSM_CHUNK // 2):
                t0 = (c * SSM_CHUNK + 2 * kp) * batch
                s0 = utm_ref[t0:t0 + batch, j * LANES:(j + 1) * LANES]
                s1 = utm_ref[t0 + batch:t0 + 2 * batch, j * LANES:(j + 1) * LANES]
                col = (2 * j) * tile + kp * LANES
                x_ref[r0:r0 + batch, col:col + LANES] = jnp.where(lo, s0, pltpu.roll(s1, half, 1))
                col += tile
                x_ref[r0:r0 + batch, col:col + LANES] = jnp.where(lo, pltpu.roll(s0, half, 1), s1)

    def chunk_inputs(j):
        for q in (2 * j, 2 * j + 1):
            xq = x_ref[:, q * tile:(q + 1) * tile].astype(BF16)
            st_ref[:, q * tile:(q + 1) * tile] = _dot(xq, ere_ref[q])
            st_ref[:, n_state + q * tile:n_state + (q + 1) * tile] = _dot(xq, eim_ref[q])

    def recurrence(j):
        re = slice(j * 2 * tile, (j + 1) * 2 * tile)
        im = slice(n_state + j * 2 * tile, n_state + (j + 1) * 2 * tile)
        l_re = lcre_ref[:, re]
        l_im = lcim_ref[:, re]
        h_re = hst_ref[:, re]
        h_im = hst_ref[:, im]
        for c in range(n_c):
            rows = slice(c * batch, (c + 1) * batch)
            b_re = st_ref[rows, re]
            b_im = st_ref[rows, im]
            st_ref[rows, re] = h_re
            st_ref[rows, im] = h_im
            h_re, h_im = (l_re * h_re - l_im * h_im + b_re, l_re * h_im + l_im * h_re + b_im)
        hst_ref[:, re] = h_re
        hst_ref[:, im] = h_im

    def chunk_outputs(j):
        for q in (2 * j, 2 * j + 1):
            cols = slice(q * tile, (q + 1) * tile)
            xq = x_ref[:, cols].astype(BF16)
            h_re = st_ref[:, cols].astype(BF16)
            h_im = st_ref[:, n_state + q * tile:n_state + (q + 1) * tile].astype(BF16)
            yc_ref[:, cols] = (_dot(h_re, fre_ref[q]) + _dot(h_im, fim_ref[q])
                               + _dot(xq, tz_ref[q]))

    def scatter(j):
        for c in range(n_c):
            r0 = c * batch
            for rp in range(SSM_CHUNK // 2):
                col = (2 * j) * tile + rp * LANES
                a = _gelu_tanh(yc_ref[r0:r0 + batch, col:col + LANES])
                b = _gelu_tanh(yc_ref[r0:r0 + batch, col + tile:col + tile + LANES])
                t0 = (c * SSM_CHUNK + 2 * rp) * batch
                ytm_ref[slot_a, t0:t0 + batch, j * LANES:(j + 1) * LANES] = jnp.where(
                    lo, a, pltpu.roll(b, half, 1))
                ytm_ref[slot_a, t0 + batch:t0 + 2 * batch, j * LANES:(j + 1) * LANES] = jnp.where(
                    lo, pltpu.roll(a, half, 1), b)

    phases = (gather, chunk_inputs, recurrence, chunk_outputs, scatter)
    assert n_pairs == 4 and n_state == n_q * tile, "the schedule below is written for 4 quad pairs"
    utm_ref[...] = jnp.swapaxes(u_ref[...].astype(F32), 0, 1).reshape(n_t * batch, sw)
    for tick in range(n_pairs + len(phases) - 1):
        for depth, phase in enumerate(phases):
            j = tick - depth
            if 0 <= j < n_pairs:
                phase(j)
        if tick % 2 == 1:
            post(tick // 2)


def _s5(u, ere, eim, lcre, lcim, fre, fim, tz, glu_w, glu_b, ssm_p):
    batch, s, sw = u.shape
    d = ssm_p.shape[1]
    n_state = lcre.shape[1]
    n_t = TIME_TILE
    n_blocks = s // n_t
    rows_c = (n_t // SSM_CHUNK) * batch
    kern = functools.partial(_s5_kernel, n_state=n_state)
    consts = (ere, eim, lcre, lcim, fre, fim, tz, glu_w, glu_b, ssm_p)
    return pl.pallas_call(
        kern,
        grid=(n_blocks + 1,),
        in_specs=[pl.BlockSpec((batch, n_t, sw), lambda i: (0, jnp.minimum(i, n_blocks - 1), 0))]
                 + [_const_spec(c.shape) for c in consts],
        out_specs=pl.BlockSpec((batch, n_t, d), lambda i: (0, jnp.maximum(i - 1, 0), 0)),
        out_shape=jax.ShapeDtypeStruct((batch, s, d), BF16),
        scratch_shapes=[pltpu.VMEM((batch * n_t, sw), F32),
                        pltpu.VMEM((rows_c, SSM_CHUNK * sw), F32),
                        pltpu.VMEM((rows_c, 2 * n_state), F32),
                        pltpu.VMEM((batch, 2 * n_state), F32),
                        pltpu.VMEM((rows_c, SSM_CHUNK * sw), F32),
                        pltpu.VMEM((2, batch * n_t, sw), F32)],
        compiler_params=pltpu.CompilerParams(
            dimension_semantics=("arbitrary",),
            vmem_limit_bytes=VMEM_LIMIT_BYTES),
        name="s5",
    )(u, *consts)


def _ffn_kernel(x_ref, part_ref, sgs_ref, yssm_ref, wo_ref, g2_ref, wg_ref, wu_ref, wd_ref,
                gf_ref, o_ref, *, chunks):
    tb = x_ref.shape[0] // FFN_SUBTILES
    subs = [slice(sub * tb, (sub + 1) * tb) for sub in range(FFN_SUBTILES)]
    x1s = []
    for rows in subs:
        merged = (part_ref[rows, :].astype(F32)
                  + sgs_ref[rows, :].astype(F32) * yssm_ref[rows, :].astype(F32))
        x1s.append(x_ref[rows, :] + _dot(merged.astype(BF16), wo_ref[...]))
    h2s = [_rms(x1, g2_ref[...]).astype(BF16) for x1 in x1s]
    outs = []
    for x1, h2 in zip(x1s, h2s):
        acc = x1
        for lo, hi in chunks:
            g = _dot(h2, wg_ref[:, lo:hi])
            a = (g * _sigmoid(g) * _dot(h2, wu_ref[:, lo:hi])).astype(BF16)
            acc = acc + _dot(a, wd_ref[lo:hi, :])
        outs.append(_rms(acc, gf_ref[...]))
    o_ref[...] = jnp.concatenate(outs, axis=0)


def _ffn_chunks(hidden):
    step = 4 * MXU_TILE
    return tuple((lo, min(lo + step, hidden)) for lo in range(0, hidden, step))


def _ffn(x, part, sgs, yssm, w_o, g2, wg, wu, wd, gf):
    b, s, d = x.shape
    tb = FFN_ROW_TILE
    kern = functools.partial(_ffn_kernel, chunks=_ffn_chunks(wg.shape[1]))
    row_spec = pl.BlockSpec((None, tb, d), lambda bi, i: (bi, i, 0))
    return pl.pallas_call(
        kern,
        grid=(b, s // tb),
        in_specs=[row_spec, row_spec, row_spec, row_spec,
                  _const_spec(w_o.shape), _const_spec((1, d)),
                  _const_spec(wg.shape), _const_spec(wu.shape), _const_spec(wd.shape),
                  _const_spec((1, d))],
        out_specs=row_spec,
        out_shape=jax.ShapeDtypeStruct((b, s, d), x.dtype),
        compiler_params=pltpu.CompilerParams(
            dimension_semantics=("arbitrary", "arbitrary"),
            vmem_limit_bytes=FFN_VMEM_LIMIT_BYTES),
        name="ffn",
    )(x, part, sgs, yssm, w_o, g2, wg, wu, wd, gf)


def _pack_toeplitz(kq, d_skip, h):
    n, n_q, qc, _ = kq.shape
    idx = jnp.arange(qc)
    same_group = (idx[:, None] // h == idx[None, :] // h).astype(F32)
    blk = kq * same_group
    blk = blk.at[0].add(jnp.eye(qc, dtype=F32) * d_skip.reshape(n_q, 1, qc))
    zero = jnp.zeros_like(blk[0])
    rows = [jnp.concatenate([blk[r - k] if r >= k else zero for r in range(n)], axis=-1)
            for k in range(n)]
    return jnp.concatenate(rows, axis=1)


def kernel(x, mem, norm1_g, w_in, conv_w, conv_proj, ssm_A_re, ssm_A_im, ssm_log_dt, ssm_B_re,
           ssm_B_im, ssm_C_re, ssm_C_im, ssm_D, ssm_glu_w, ssm_glu_b, ssm_proj, mem_norm_g,
           attn_wk, attn_wv, attn_proj, w_o, norm2_g, ffn_w_gate, ffn_w_up, ffn_w_down,
           final_norm_g):
    b, s, d = x.shape
    depth = w_in.shape[0]
    cw = conv_w.shape[2]
    sw = ssm_D.shape[1]
    mw = attn_wk.shape[2]
    n_groups = ssm_A_re.shape[1]
    n_state = n_groups * SSM_STATE
    assert depth == 1, "the final norm is fused into the single layer's ffn kernel"
    assert b == SUBLANES, "the S5 recurrence keeps the batch on the sublane axis"

    for l in range(depth):
        lc_re, lc_im, e_re, e_im, f_re, f_im, kq = _ssm_prep(
            ssm_A_re[l], ssm_A_im[l], ssm_log_dt[l],
            ssm_B_re[l].transpose(0, 2, 1), ssm_B_im[l].transpose(0, 2, 1),
            ssm_C_re[l], ssm_C_im[l])
        ere = e_re.astype(BF16)
        eim = e_im.astype(BF16)
        fre = f_re.transpose(0, 2, 1).astype(BF16)
        fim = f_im.transpose(0, 2, 1).astype(BF16)
        tz = _pack_toeplitz(kq, ssm_D[l], SSM_GROUP).astype(BF16)
        lcre = jnp.broadcast_to(lc_re.reshape(1, n_state), (b, n_state))
        lcim = jnp.broadcast_to(lc_im.reshape(1, n_state), (b, n_state))

        kt, v = _mem_kv(mem, mem_norm_g[l].reshape(1, d), attn_wk[l].T.astype(BF16),
                        attn_wv[l].astype(BF16))
        u, sgs, part = _mixer(
            x, norm1_g[l].reshape(1, d), w_in[l].astype(BF16), conv_w[l],
            conv_proj[l].astype(BF16), kt, v, attn_proj[l].astype(BF16), cw=cw, sw=sw, mw=mw)
        yssm = _s5(u, ere, eim, lcre, lcim, fre, fim, tz,
                   ssm_glu_w[l].astype(BF16), ssm_glu_b[l].reshape(1, sw),
                   ssm_proj[l].astype(BF16))
        x = _ffn(x, part, sgs, yssm, w_o[l].astype(BF16),
                 norm2_g[l].reshape(1, d), ffn_w_gate[l].astype(BF16), ffn_w_up[l].astype(BF16),
                 ffn_w_down[l].astype(BF16), final_norm_g.reshape(1, d))
    return x
```

```python
import functools
import math

import jax
import jax.numpy as jnp
from jax import lax
from jax.experimental import pallas as pl
from jax.experimental.pallas import tpu as pltpu

F32 = jnp.float32
BF16 = jnp.bfloat16

EPS = 1e-6
SSM_GROUP = 16
SSM_STATE = 64
MEM_HEADS = 4
MEM_HEAD_DIM = 128

MXU_TILE = 256
SUBLANES = 8
LANES = 128
VMEM_LIMIT_BYTES = 56 * 1024 * 1024
FFN_VMEM_LIMIT_BYTES = 62 * 1024 * 1024

MIXER_ROW_TILE = 1024
FFN_ROW_TILE = 1024
FFN_SUBTILES = 4
TIME_TILE = 256
SSM_CHUNK = 4
QUAD_CH = MXU_TILE // SSM_CHUNK
POST_ROWS = 512


def _rms(x, g):
    ms = jnp.mean(x * x, axis=-1, keepdims=True)
    return x * lax.rsqrt(ms + EPS) * g


def _dot(a, b):
    return jnp.dot(a, b, preferred_element_type=F32)


def _gelu_tanh(x):
    k0 = -2.0 * math.sqrt(2.0 / math.pi) * math.log2(math.e)
    w = x * (k0 + (k0 * 0.044715) * (x * x))
    return x / (1.0 + jnp.exp2(w))


def _sigmoid(x):
    return 0.5 + 0.5 * jnp.tanh(0.5 * x)


def _const_spec(shape):
    nd = len(shape)
    return pl.BlockSpec(shape, lambda *_: (0,) * nd, pipeline_mode=pl.Buffered(1))


def _ssm_prep_kernel(are_ref, aim_ref, ldt_ref, bre_ref, bim_ref, cre_ref, cim_ref,
                     lcre_ref, lcim_ref, ere_ref, eim_ref, fre_ref, fim_ref, kq_ref):
    a_re = are_ref[...]
    a_im = aim_ref[...]
    dt = jnp.exp(ldt_ref[...])
    mag = jnp.exp(a_re * dt)
    l_re = mag * jnp.cos(a_im * dt)
    l_im = mag * jnp.sin(a_im * dt)
    n_re = l_re - 1.0
    n_im = l_im
    den = a_re * a_re + a_im * a_im
    q_re = ((n_re * a_re + n_im * a_im) / den)[:, None, :]
    q_im = ((n_im * a_re - n_re * a_im) / den)[:, None, :]
    b_re = bre_ref[...]
    b_im = bim_ref[...]
    bb_re = q_re * b_re - q_im * b_im
    bb_im = q_re * b_im + q_im * b_re
    c_re = cre_ref[...]
    c_im = cim_ref[...]
    g, h, p = c_re.shape

    pows = [(jnp.ones_like(l_re), jnp.zeros_like(l_re))]
    for _ in range(SSM_CHUNK):
        pr, pi = pows[-1]
        pows.append((pr * l_re - pi * l_im, pr * l_im + pi * l_re))
    lcre_ref[...] = pows[SSM_CHUNK][0]
    lcim_ref[...] = pows[SSM_CHUNK][1]

    def place(out_ref, blocks):
        per_quad = QUAD_CH // h
        out_ref[...] = jnp.zeros(out_ref.shape, F32)
        for k, blk in enumerate(blocks):
            for gi in range(g):
                a = gi % per_quad
                r0 = (k * per_quad + a) * h
                out_ref[gi // per_quad, r0:r0 + h, a * p:(a + 1) * p] = blk[gi]

    e_re, e_im, f_re, f_im = [], [], [], []
    for k in range(SSM_CHUNK):
        pr, pi = pows[SSM_CHUNK - 1 - k]
        pr, pi = pr[:, None, :], pi[:, None, :]
        e_re.append(pr * bb_re - pi * bb_im)
        e_im.append(pr * bb_im + pi * bb_re)
    for r in range(SSM_CHUNK):
        pr, pi = pows[r + 1]
        pr, pi = pr[:, None, :], pi[:, None, :]
        f_re.append(c_re * pr - c_im * pi)
        f_im.append(-(c_re * pi + c_im * pr))
    place(ere_ref, e_re)
    place(eim_ref, e_im)
    place(fre_ref, f_re)
    place(fim_ref, f_im)
    nt = (((1,), (1,)), ((), ()))
    c_re2 = c_re.reshape(g * h, p)
    c_im2 = c_im.reshape(g * h, p)
    for j in range(SSM_CHUNK):
        pr, pi = pows[j]
        pr, pi = pr[:, None, :], pi[:, None, :]
        w_re = (pr * bb_re - pi * bb_im).reshape(g * h, p)
        w_im = (pr * bb_im + pi * bb_re).reshape(g * h, p)
        kk = (lax.dot_general(w_re, c_re2, nt, precision=lax.Precision.HIGHEST,
                              preferred_element_type=F32)
              - lax.dot_general(w_im, c_im2, nt, precision=lax.Precision.HIGHEST,
                                preferred_element_type=F32))
        for q in range(g * h // QUAD_CH):
            kq_ref[j, q] = kk[q * QUAD_CH:(q + 1) * QUAD_CH, q * QUAD_CH:(q + 1) * QUAD_CH]


def _ssm_prep(a_re, a_im, log_dt, b_re_t, b_im_t, c_re, c_im):
    g, p = a_re.shape
    h = b_re_t.shape[1]
    n = SSM_CHUNK
    return pl.pallas_call(
        _ssm_prep_kernel,
        out_shape=(jax.ShapeDtypeStruct((g, p), F32), jax.ShapeDtypeStruct((g, p), F32),
                   *[jax.ShapeDtypeStruct((g * h // QUAD_CH, n * QUAD_CH, n * QUAD_CH), F32)] * 4,
                   jax.ShapeDtypeStruct((n, g * h // QUAD_CH, QUAD_CH, QUAD_CH), F32)),
        name="ssm_prep",
    )(a_re, a_im, log_dt.reshape(g, 1), b_re_t, b_im_t, c_re, c_im)


def _mem_kv_kernel(mem_ref, g_ref, wkt_ref, wv_ref, kt_ref, v_ref):
    mn = _rms(mem_ref[...], g_ref[...]).astype(BF16)
    kt = lax.dot_general(wkt_ref[...], mn, (((1,), (1,)), ((), ())),
                         preferred_element_type=F32)
    kt_ref[...] = (kt * (MEM_HEAD_DIM ** -0.5 * math.log2(math.e))).astype(BF16)
    v_ref[...] = _dot(mn, wv_ref[...]).astype(BF16)


def _mem_kv(mem, g, wkt, wv):
    b, m, d = mem.shape
    w = wv.shape[1]
    return pl.pallas_call(
        _mem_kv_kernel,
        grid=(b,),
        in_specs=[pl.BlockSpec((None, m, d), lambda i: (i, 0, 0)),
                  pl.BlockSpec((1, d), lambda i: (0, 0)),
                  pl.BlockSpec((w, d), lambda i: (0, 0)),
                  pl.BlockSpec((d, w), lambda i: (0, 0))],
        out_specs=(pl.BlockSpec((None, w, m), lambda i: (i, 0, 0)),
                   pl.BlockSpec((None, m, w), lambda i: (i, 0, 0))),
        out_shape=(jax.ShapeDtypeStruct((b, w, m), BF16), jax.ShapeDtypeStruct((b, m, w), BF16)),
        name="mem_kv",
    )(mem, g, wkt, wv)


def _mixer_kernel(x_ref, g1_ref, win_ref, convw_ref, convp_ref, kt_ref, v_ref, attnp_ref,
                  u_ref, sgs_ref, part_ref, cv_ref, *, cw, sw, mw, d):
    i = pl.program_id(1)
    tb = x_ref.shape[0]
    h = _rms(x_ref[...], g1_ref[...]).astype(BF16)

    def proj(lo, hi):
        return _dot(h, win_ref[:, lo:hi])

    o_u = 3 * cw
    o_q = o_u + sw
    o_gc = o_q + mw
    o_gs = o_gc + d
    o_gm = o_gs + d

    q = proj(o_q, o_gc)
    scores = []
    for hd in range(MEM_HEADS):
        lo = hd * MEM_HEAD_DIM
        scores.append(_dot(q[:, lo:lo + MEM_HEAD_DIM].astype(BF16),
                           kt_ref[lo:lo + MEM_HEAD_DIM, :]))
    c_gate = proj(cw, 2 * cw)
    v_conv = proj(2 * cw, 3 * cw)

    heads = []
    for hd in range(MEM_HEADS):
        lo = hd * MEM_HEAD_DIM
        s = scores[hd]
        e = jnp.exp2(s - jnp.max(s, axis=-1, keepdims=True))
        pv = _dot(e.astype(BF16), v_ref[:, lo:lo + MEM_HEAD_DIM])
        heads.append(pv / jnp.sum(e, axis=-1, keepdims=True))
    o = jnp.concatenate(heads, axis=-1).astype(BF16)
    b_gate = proj(0, cw)

    cv = c_gate * v_conv

    @pl.when(i == 0)
    def _():
        cv_ref[0:SUBLANES, :] = jnp.zeros((SUBLANES, cw), F32)

    cv_ref[SUBLANES:SUBLANES + tb, :] = cv
    conv = (convw_ref[0:1, :] * cv_ref[SUBLANES - 2:SUBLANES - 2 + tb, :]
            + convw_ref[1:2, :] * cv_ref[SUBLANES - 1:SUBLANES - 1 + tb, :]
            + convw_ref[2:3, :] * cv)
    cv_ref[0:SUBLANES, :] = cv_ref[tb:tb + SUBLANES, :]
    pre = (b_gate * conv).astype(BF16)

    g_c = proj(o_gc, o_gs)
    y_mem = _dot(o, attnp_ref[...])
    y_conv = _dot(pre, convp_ref[...])
    g_m = proj(o_gm, o_gm + d)
    part_ref[...] = (_sigmoid(g_c) * y_conv + _sigmoid(g_m) * y_mem).astype(BF16)
    sgs_ref[...] = _sigmoid(proj(o_gs, o_gm)).astype(BF16)
    u_ref[...] = proj(o_u, o_q).astype(BF16)


def _mixer(x, g1, w_in, conv_w, conv_p, kt, v, attn_p, *, cw, sw, mw):
    b, s, d = x.shape
    tb = MIXER_ROW_TILE
    m = kt.shape[2]
    kern = functools.partial(_mixer_kernel, cw=cw, sw=sw, mw=mw, d=d)
    return pl.pallas_call(
        kern,
        grid=(b, s // tb),
        in_specs=[pl.BlockSpec((None, tb, d), lambda bi, i: (bi, i, 0)),
                  _const_spec((1, d)),
                  _const_spec(w_in.shape),
                  _const_spec(conv_w.shape),
                  _const_spec(conv_p.shape),
                  pl.BlockSpec((None, mw, m), lambda bi, i: (bi, 0, 0)),
                  pl.BlockSpec((None, m, mw), lambda bi, i: (bi, 0, 0)),
                  _const_spec(attn_p.shape)],
        out_specs=(pl.BlockSpec((None, tb, sw), lambda bi, i: (bi, i, 0)),
                   pl.BlockSpec((None, tb, d), lambda bi, i: (bi, i, 0)),
                   pl.BlockSpec((None, tb, d), lambda bi, i: (bi, i, 0))),
        out_shape=(jax.ShapeDtypeStruct((b, s, sw), BF16),
                   jax.ShapeDtypeStruct((b, s, d), BF16),
                   jax.ShapeDtypeStruct((b, s, d), BF16)),
        scratch_shapes=[pltpu.VMEM((tb + SUBLANES, cw), F32)],
        compiler_params=pltpu.CompilerParams(
            dimension_semantics=("arbitrary", "arbitrary"),
            vmem_limit_bytes=VMEM_LIMIT_BYTES),
        name="mixer",
    )(x, g1, w_in, conv_w, conv_p, kt, v, attn_p)


def _s5_kernel(u_ref, ere_ref, eim_ref, lcre_ref, lcim_ref, fre_ref, fim_ref, tz_ref,
               gw_ref, gb_ref, sp_ref, y_ref,
               utm_ref, x_ref, st_ref, hst_ref, yc_ref, ytm_ref, *, n_state):
    batch, n_t, sw = u_ref.shape
    n_c = n_t // SSM_CHUNK
    n_q = sw // QUAD_CH
    tile = SSM_CHUNK * QUAD_CH
    half = LANES // 2
    lo = lax.broadcasted_iota(jnp.int32, (batch, LANES), 1) < half
    step_i = pl.program_id(0)
    slot_a = step_i % 2
    slot_b = 1 - slot_a

    @pl.when(step_i == 0)
    def _():
        hst_ref[...] = jnp.zeros_like(hst_ref)
        ytm_ref[1] = jnp.zeros(ytm_ref.shape[1:], F32)

    sub_t = POST_ROWS // batch

    def post(sb):
        y = ytm_ref[slot_b, sb * POST_ROWS:(sb + 1) * POST_ROWS, :].reshape(sub_t, batch, sw)
        y = jnp.swapaxes(y, 0, 1).reshape(POST_ROWS, sw)
        half_y = 0.5 * y
        y = half_y + half_y * jnp.tanh(0.5 * (_dot(y.astype(BF16), gw_ref[...]) + gb_ref[...]))
        out = _dot(y.astype(BF16), sp_ref[...]).astype(BF16)
        y_ref[:, sb * sub_t:(sb + 1) * sub_t, :] = out.reshape(batch, sub_t, -1)

    assert n_t // sub_t == 4, "stage B is interleaved as four sub-blocks"

    n_pairs = n_q // 2

    def gather(j):
        for c in range(n_c):
            r0 = c * batch
            for kp in range(SSM_CHUNK // 2):
                t0 = (c * SSM_CHUNK + 2 * kp) * batch
                s0 = utm_ref[t0:t0 + batch, j * LANES:(j + 1) * LANES]
                s1 = utm_ref[t0 + batch:t0 + 2 * batch, j * LANES:(j + 1) * LANES]
                col = (2 * j) * tile + kp * LANES
                x_ref[r0:r0 + batch, col:col + LANES] = jnp.where(lo, s0, pltpu.roll(s1, half, 1))
                col += tile
                x_ref[r0:r0 + batch, col:col + LANES] = jnp.where(lo, pltpu.roll(s0, half, 1), s1)

    def chunk_inputs(j):
        for q in (2 * j, 2 * j + 1):
            xq = x_ref[:, q * tile:(q + 1) * tile].astype(BF16)
            st_ref[:, q * tile:(q + 1) * tile] = _dot(xq, ere_ref[q])
            st_ref[:, n_state + q * tile:n_state + (q + 1) * tile] = _dot(xq, eim_ref[q])

    def recurrence(j):
        re = slice(j * 2 * tile, (j + 1) * 2 * tile)
        im = slice(n_state + j * 2 * tile, n_state + (j + 1) * 2 * tile)
        l_re = lcre_ref[:, re]
        l_im = lcim_ref[:, re]
        h_re = hst_ref[:, re]
        h_im = hst_ref[:, im]
        for c in range(n_c):
            rows = slice(c * batch, (c + 1) * batch)
            b_re = st_ref[rows, re]
            b_im = st_ref[rows, im]
            st_ref[rows, re] = h_re
            st_ref[rows, im] = h_im
            h_re, h_im = (l_re * h_re - l_im * h_im + b_re, l_re * h_im + l_im * h_re + b_im)
        hst_ref[:, re] = h_re
        hst_ref[:, im] = h_im

    def chunk_outputs(j):
        for q in (2 * j, 2 * j + 1):
            cols = slice(q * tile, (q + 1) * tile)
            xq = x_ref[:, cols].astype(BF16)
            h_re = st_ref[:, cols].astype(BF16)
            h_im = st_ref[:, n_state + q * tile:n_state + (q + 1) * tile].astype(BF16)
            yc_ref[:, cols] = (_dot(h_re, fre_ref[q]) + _dot(h_im, fim_ref[q])
                               + _dot(xq, tz_ref[q]))

    def scatter(j):
        for c in range(n_c):
            r0 = c * batch
            for rp in range(SSM_CHUNK // 2):
                col = (2 * j) * tile + rp * LANES
                a = _gelu_tanh(yc_ref[r0:r0 + batch, col:col + LANES])
                b = _gelu_tanh(yc_ref[r0:r0 + batch, col + tile:col + tile + LANES])
                t0 = (c * SSM_CHUNK + 2 * rp) * batch
                ytm_ref[slot_a, t0:t0 + batch, j * LANES:(j + 1) * LANES] = jnp.where(
                    lo, a, pltpu.roll(b, half, 1))
                ytm_ref[slot_a, t0 + batch:t0 + 2 * batch, j * LANES:(j + 1) * LANES] = jnp.where(
                    lo, pltpu.roll(a, half, 1), b)

    phases = (gather, chunk_inputs, recurrence, chunk_outputs, scatter)
    assert n_pairs == 4 and n_state == n_q * tile, "the schedule below is written for 4 quad pairs"
    utm_ref[...] = jnp.swapaxes(u_ref[...].astype(F32), 0, 1).reshape(n_t * batch, sw)
    for tick in range(n_pairs + len(phases) - 1):
        for depth, phase in enumerate(phases):
            j = tick - depth
            if 0 <= j < n_pairs:
                phase(j)
        if tick % 2 == 1:
            post(tick // 2)


def _s5(u, ere, eim, lcre, lcim, fre, fim, tz, glu_w, glu_b, ssm_p):
    batch, s, sw = u.shape
    d = ssm_p.shape[1]
    n_state = lcre.shape[1]
    n_t = TIME_TILE
    n_blocks = s // n_t
    rows_c = (n_t // SSM_CHUNK) * batch
    kern = functools.partial(_s5_kernel, n_state=n_state)
    consts = (ere, eim, lcre, lcim, fre, fim, tz, glu_w, glu_b, ssm_p)
    return pl.pallas_call(
        kern,
        grid=(n_blocks + 1,),
        in_specs=[pl.BlockSpec((batch, n_t, sw), lambda i: (0, jnp.minimum(i, n_blocks - 1), 0))]
                 + [_const_spec(c.shape) for c in consts],
        out_specs=pl.BlockSpec((batch, n_t, d), lambda i: (0, jnp.maximum(i - 1, 0), 0)),
        out_shape=jax.ShapeDtypeStruct((batch, s, d), BF16),
        scratch_shapes=[pltpu.VMEM((batch * n_t, sw), F32),
                        pltpu.VMEM((rows_c, SSM_CHUNK * sw), F32),
                        pltpu.VMEM((rows_c, 2 * n_state), F32),
                        pltpu.VMEM((batch, 2 * n_state), F32),
                        pltpu.VMEM((rows_c, SSM_CHUNK * sw), F32),
                        pltpu.VMEM((2, batch * n_t, sw), F32)],
        compiler_params=pltpu.CompilerParams(
            dimension_semantics=("arbitrary",),
            vmem_limit_bytes=VMEM_LIMIT_BYTES),
        name="s5",
    )(u, *consts)


def _ffn_kernel(x_ref, part_ref, sgs_ref, yssm_ref, wo_ref, g2_ref, wg_ref, wu_ref, wd_ref,
                gf_ref, o_ref, *, chunks):
    tb = x_ref.shape[0] // FFN_SUBTILES
    subs = [slice(sub * tb, (sub + 1) * tb) for sub in range(FFN_SUBTILES)]
    x1s = []
    for rows in subs:
        merged = (part_ref[rows, :].astype(F32)
                  + sgs_ref[rows, :].astype(F32) * yssm_ref[rows, :].astype(F32))
        x1s.append(x_ref[rows, :] + _dot(merged.astype(BF16), wo_ref[...]))
    h2s = [_rms(x1, g2_ref[...]).astype(BF16) for x1 in x1s]
    outs = []
    for x1, h2 in zip(x1s, h2s):
        acc = x1
        for lo, hi in chunks:
            g = _dot(h2, wg_ref[:, lo:hi])
            a = (g * _sigmoid(g) * _dot(h2, wu_ref[:, lo:hi])).astype(BF16)
            acc = acc + _dot(a, wd_ref[lo:hi, :])
        outs.append(_rms(acc, gf_ref[...]))
    o_ref[...] = jnp.concatenate(outs, axis=0)


def _ffn_chunks(hidden):
    step = 4 * MXU_TILE
    return tuple((lo, min(lo + step, hidden)) for lo in range(0, hidden, step))


def _ffn(x, part, sgs, yssm, w_o, g2, wg, wu, wd, gf):
    b, s, d = x.shape
    tb = FFN_ROW_TILE
    kern = functools.partial(_ffn_kernel, chunks=_ffn_chunks(wg.shape[1]))
    row_spec = pl.BlockSpec((None, tb, d), lambda bi, i: (bi, i, 0))
    return pl.pallas_call(
        kern,
        grid=(b, s // tb),
        in_specs=[row_spec, row_spec, row_spec, row_spec,
                  _const_spec(w_o.shape), _const_spec((1, d)),
                  _const_spec(wg.shape), _const_spec(wu.shape), _const_spec(wd.shape),
                  _const_spec((1, d))],
        out_specs=row_spec,
        out_shape=jax.ShapeDtypeStruct((b, s, d), x.dtype),
        compiler_params=pltpu.CompilerParams(
            dimension_semantics=("arbitrary", "arbitrary"),
            vmem_limit_bytes=FFN_VMEM_LIMIT_BYTES),
        name="ffn",
    )(x, part, sgs, yssm, w_o, g2, wg, wu, wd, gf)


def _pack_toeplitz(kq, d_skip, h):
    n, n_q, qc, _ = kq.shape
    idx = jnp.arange(qc)
    same_group = (idx[:, None] // h == idx[None, :] // h).astype(F32)
    blk = kq * same_group
    blk = blk.at[0].add(jnp.eye(qc, dtype=F32) * d_skip.reshape(n_q, 1, qc))
    zero = jnp.zeros_like(blk[0])
    rows = [jnp.concatenate([blk[r - k] if r >= k else zero for r in range(n)], axis=-1)
            for k in range(n)]
    return jnp.concatenate(rows, axis=1)


def kernel(x, mem, norm1_g, w_in, conv_w, conv_proj, ssm_A_re, ssm_A_im, ssm_log_dt, ssm_B_re,
           ssm_B_im, ssm_C_re, ssm_C_im, ssm_D, ssm_glu_w, ssm_glu_b, ssm_proj, mem_norm_g,
           attn_wk, attn_wv, attn_proj, w_o, norm2_g, ffn_w_gate, ffn_w_up, ffn_w_down,
           final_norm_g):
    b, s, d = x.shape
    depth = w_in.shape[0]
    cw = conv_w.shape[2]
    sw = ssm_D.shape[1]
    mw = attn_wk.shape[2]
    n_groups = ssm_A_re.shape[1]
    n_state = n_groups * SSM_STATE
    assert depth == 1, "the final norm is fused into the single layer's ffn kernel"
    assert b == SUBLANES, "the S5 recurrence keeps the batch on the sublane axis"

    for l in range(depth):
        lc_re, lc_im, e_re, e_im, f_re, f_im, kq = _ssm_prep(
            ssm_A_re[l], ssm_A_im[l], ssm_log_dt[l],
            ssm_B_re[l].transpose(0, 2, 1), ssm_B_im[l].transpose(0, 2, 1),
            ssm_C_re[l], ssm_C_im[l])
        ere = e_re.astype(BF16)
        eim = e_im.astype(BF16)
        fre = f_re.transpose(0, 2, 1).astype(BF16)
        fim = f_im.transpose(0, 2, 1).astype(BF16)
        tz = _pack_toeplitz(kq, ssm_D[l], SSM_GROUP).astype(BF16)
        lcre = jnp.broadcast_to(lc_re.reshape(1, n_state), (b, n_state))
        lcim = jnp.broadcast_to(lc_im.reshape(1, n_state), (b, n_state))

        kt, v = _mem_kv(mem, mem_norm_g[l].reshape(1, d), attn_wk[l].T.astype(BF16),
                        attn_wv[l].astype(BF16))
        u, sgs, part = _mixer(
            x, norm1_g[l].reshape(1, d), w_in[l].astype(BF16), conv_w[l],
            conv_proj[l].astype(BF16), kt, v, attn_proj[l].astype(BF16), cw=cw, sw=sw, mw=mw)
        yssm = _s5(u, ere, eim, lcre, lcim, fre, fim, tz,
                   ssm_glu_w[l].astype(BF16), ssm_glu_b[l].reshape(1, sw),
                   ssm_proj[l].astype(BF16))
        x = _ffn(x, part, sgs, yssm, w_o[l].astype(BF16),
                 norm2_g[l].reshape(1, d), ffn_w_gate[l].astype(BF16), ffn_w_up[l].astype(BF16),
                 ffn_w_down[l].astype(BF16), final_norm_g.reshape(1, d))
    return x
```

```python
import functools
import math

import jax
import jax.numpy as jnp
from jax import lax
from jax.experimental import pallas as pl
from jax.experimental.pallas import tpu as pltpu

F32 = jnp.float32
BF16 = jnp.bfloat16

EPS = 1e-6
SSM_GROUP = 16
SSM_STATE = 64
MEM_HEADS = 4
MEM_HEAD_DIM = 128

MXU_TILE = 256
SUBLANES = 8
LANES = 128
VMEM_LIMIT_BYTES = 56 * 1024 * 1024
FFN_VMEM_LIMIT_BYTES = 62 * 1024 * 1024

MIXER_ROW_TILE = 1024
FFN_ROW_TILE = 1024
FFN_SUBTILES = 4
TIME_TILE = 256
SSM_CHUNK = 4
QUAD_CH = MXU_TILE // SSM_CHUNK
POST_ROWS = 512


def _rms(x, g):
    ms = jnp.mean(x * x, axis=-1, keepdims=True)
    return x * lax.rsqrt(ms + EPS) * g


def _dot(a, b):
    return jnp.dot(a, b, preferred_element_type=F32)


def _gelu_tanh(x):
    k0 = -2.0 * math.sqrt(2.0 / math.pi) * math.log2(math.e)
    w = x * (k0 + (k0 * 0.044715) * (x * x))
    return x / (1.0 + jnp.exp2(w))


def _const_spec(shape):
    nd = len(shape)
    return pl.BlockSpec(shape, lambda *_: (0,) * nd, pipeline_mode=pl.Buffered(1))


def _ssm_prep_kernel(are_ref, aim_ref, ldt_ref, bre_ref, bim_ref, cre_ref, cim_ref,
                     lcre_ref, lcim_ref, ere_ref, eim_ref, fre_ref, fim_ref, kq_ref):
    a_re = are_ref[...]
    a_im = aim_ref[...]
    dt = jnp.exp(ldt_ref[...])
    mag = jnp.exp(a_re * dt)
    l_re = mag * jnp.cos(a_im * dt)
    l_im = mag * jnp.sin(a_im * dt)
    n_re = l_re - 1.0
    n_im = l_im
    den = a_re * a_re + a_im * a_im
    q_re = ((n_re * a_re + n_im * a_im) / den)[:, None, :]
    q_im = ((n_im * a_re - n_re * a_im) / den)[:, None, :]
    b_re = bre_ref[...]
    b_im = bim_ref[...]
    bb_re = q_re * b_re - q_im * b_im
    bb_im = q_re * b_im + q_im * b_re
    c_re = cre_ref[...]
    c_im = cim_ref[...]
    g, h, p = c_re.shape

    pows = [(jnp.ones_like(l_re), jnp.zeros_like(l_re))]
    for _ in range(SSM_CHUNK):
        pr, pi = pows[-1]
        pows.append((pr * l_re - pi * l_im, pr * l_im + pi * l_re))
    lcre_ref[...] = pows[SSM_CHUNK][0]
    lcim_ref[...] = pows[SSM_CHUNK][1]

    def place(out_ref, blocks):
        per_quad = QUAD_CH // h
        out_ref[...] = jnp.zeros(out_ref.shape, F32)
        for k, blk in enumerate(blocks):
            for gi in range(g):
                a = gi % per_quad
                r0 = (k * per_quad + a) * h
                out_ref[gi // per_quad, r0:r0 + h, a * p:(a + 1) * p] = blk[gi]

    e_re, e_im, f_re, f_im = [], [], [], []
    for k in range(SSM_CHUNK):
        pr, pi = pows[SSM_CHUNK - 1 - k]
        pr, pi = pr[:, None, :], pi[:, None, :]
        e_re.append(pr * bb_re - pi * bb_im)
        e_im.append(pr * bb_im + pi * bb_re)
    for r in range(SSM_CHUNK):
        pr, pi = pows[r + 1]
        pr, pi = pr[:, None, :], pi[:, None, :]
        f_re.append(c_re * pr - c_im * pi)
        f_im.append(-(c_re * pi + c_im * pr))
    place(ere_ref, e_re)
    place(eim_ref, e_im)
    place(fre_ref, f_re)
    place(fim_ref, f_im)
    nt = (((1,), (1,)), ((), ()))
    c_re2 = c_re.reshape(g * h, p)
    c_im2 = c_im.reshape(g * h, p)
    for j in range(SSM_CHUNK):
        pr, pi = pows[j]
        pr, pi = pr[:, None, :], pi[:, None, :]
        w_re = (pr * bb_re - pi * bb_im).reshape(g * h, p)
        w_im = (pr * bb_im + pi * bb_re).reshape(g * h, p)
        kk = (lax.dot_general(w_re, c_re2, nt, precision=lax.Precision.HIGHEST,
                              preferred_element_type=F32)
              - lax.dot_general(w_im, c_im2, nt, precision=lax.Precision.HIGHEST,
                                preferred_element_type=F32))
        for q in range(g * h // QUAD_CH):
            kq_ref[j, q] = kk[q * QUAD_CH:(q + 1) * QUAD_CH, q * QUAD_CH:(q + 1) * QUAD_CH]


def _ssm_prep(a_re, a_im, log_dt, b_re_t, b_im_t, c_re, c_im):
    g, p = a_re.shape
    h = b_re_t.shape[1]
    n = SSM_CHUNK
    return pl.pallas_call(
        _ssm_prep_kernel,
        out_shape=(jax.ShapeDtypeStruct((g, p), F32), jax.ShapeDtypeStruct((g, p), F32),
                   *[jax.ShapeDtypeStruct((g * h // QUAD_CH, n * QUAD_CH, n * QUAD_CH), F32)] * 4,
                   jax.ShapeDtypeStruct((n, g * h // QUAD_CH, QUAD_CH, QUAD_CH), F32)),
        name="ssm_prep",
    )(a_re, a_im, log_dt.reshape(g, 1), b_re_t, b_im_t, c_re, c_im)


def _mem_kv_kernel(mem_ref, g_ref, wkt_ref, wv_ref, kt_ref, v_ref):
    mn = _rms(mem_ref[...], g_ref[...]).astype(BF16)
    kt = lax.dot_general(wkt_ref[...], mn, (((1,), (1,)), ((), ())),
                         preferred_element_type=F32)
    kt_ref[...] = (kt * (MEM_HEAD_DIM ** -0.5 * math.log2(math.e))).astype(BF16)
    v_ref[...] = _dot(mn, wv_ref[...]).astype(BF16)


def _mem_kv(mem, g, wkt, wv):
    b, m, d = mem.shape
    w = wv.shape[1]
    return pl.pallas_call(
        _mem_kv_kernel,
        grid=(b,),
        in_specs=[pl.BlockSpec((None, m, d), lambda i: (i, 0, 0)),
                  pl.BlockSpec((1, d), lambda i: (0, 0)),
                  pl.BlockSpec((w, d), lambda i: (0, 0)),
                  pl.BlockSpec((d, w), lambda i: (0, 0))],
        out_specs=(pl.BlockSpec((None, w, m), lambda i: (i, 0, 0)),
                   pl.BlockSpec((None, m, w), lambda i: (i, 0, 0))),
        out_shape=(jax.ShapeDtypeStruct((b, w, m), BF16), jax.ShapeDtypeStruct((b, m, w), BF16)),
        name="mem_kv",
    )(mem, g, wkt, wv)


def _mixer_kernel(x_ref, g1_ref, win_ref, convw_ref, convp_ref, kt_ref, v_ref, attnp_ref,
                  u_ref, sgs_ref, part_ref, cv_ref, *, cw, sw, mw, d):
    i = pl.program_id(1)
    tb = x_ref.shape[0]
    h = _rms(x_ref[...], g1_ref[...]).astype(BF16)

    def proj(lo, hi):
        return _dot(h, win_ref[:, lo:hi])

    o_u = 3 * cw
    o_q = o_u + sw
    o_gc = o_q + mw
    o_gs = o_gc + d
    o_gm = o_gs + d

    q = proj(o_q, o_gc)
    scores = []
    for hd in range(MEM_HEADS):
        lo = hd * MEM_HEAD_DIM
        scores.append(_dot(q[:, lo:lo + MEM_HEAD_DIM].astype(BF16),
                           kt_ref[lo:lo + MEM_HEAD_DIM, :]))
    c_gate = proj(cw, 2 * cw)
    v_conv = proj(2 * cw, 3 * cw)

    heads = []
    for hd in range(MEM_HEADS):
        lo = hd * MEM_HEAD_DIM
        s = scores[hd]
        e = jnp.exp2(s - jnp.max(s, axis=-1, keepdims=True))
        pv = _dot(e.astype(BF16), v_ref[:, lo:lo + MEM_HEAD_DIM])
        heads.append(pv / jnp.sum(e, axis=-1, keepdims=True))
    o = jnp.concatenate(heads, axis=-1).astype(BF16)
    b_gate = proj(0, cw)

    cv = c_gate * v_conv

    @pl.when(i == 0)
    def _():
        cv_ref[0:SUBLANES, :] = jnp.zeros((SUBLANES, cw), F32)

    cv_ref[SUBLANES:SUBLANES + tb, :] = cv
    conv = (convw_ref[0:1, :] * cv_ref[SUBLANES - 2:SUBLANES - 2 + tb, :]
            + convw_ref[1:2, :] * cv_ref[SUBLANES - 1:SUBLANES - 1 + tb, :]
            + convw_ref[2:3, :] * cv)
    cv_ref[0:SUBLANES, :] = cv_ref[tb:tb + SUBLANES, :]
    pre = (b_gate * conv).astype(BF16)

    t_c = jnp.tanh(proj(o_gc, o_gs))
    y_mem = _dot(o, attnp_ref[...])
    y_conv = _dot(pre, convp_ref[...])
    t_m = jnp.tanh(proj(o_gm, o_gm + d))
    part_ref[...] = ((y_conv + y_mem) + (t_c * y_conv + t_m * y_mem)).astype(BF16)
    sgs_ref[...] = (0.5 + 0.5 * jnp.tanh(proj(o_gs, o_gm))).astype(BF16)
    u_ref[...] = proj(o_u, o_q).astype(BF16)


def _mixer(x, g1, w_in, conv_w, conv_p, kt, v, attn_p, *, cw, sw, mw):
    b, s, d = x.shape
    tb = MIXER_ROW_TILE
    m = kt.shape[2]
    kern = functools.partial(_mixer_kernel, cw=cw, sw=sw, mw=mw, d=d)
    return pl.pallas_call(
        kern,
        grid=(b, s // tb),
        in_specs=[pl.BlockSpec((None, tb, d), lambda bi, i: (bi, i, 0)),
                  _const_spec((1, d)),
                  _const_spec(w_in.shape),
                  _const_spec(conv_w.shape),
                  _const_spec(conv_p.shape),
                  pl.BlockSpec((None, mw, m), lambda bi, i: (bi, 0, 0)),
                  pl.BlockSpec((None, m, mw), lambda bi, i: (bi, 0, 0)),
                  _const_spec(attn_p.shape)],
        out_specs=(pl.BlockSpec((None, tb, sw), lambda bi, i: (bi, i, 0)),
                   pl.BlockSpec((None, tb, d), lambda bi, i: (bi, i, 0)),
                   pl.BlockSpec((None, tb, d), lambda bi, i: (bi, i, 0))),
        out_shape=(jax.ShapeDtypeStruct((b, s, sw), BF16),
                   jax.ShapeDtypeStruct((b, s, d), BF16),
                   jax.ShapeDtypeStruct((b, s, d), BF16)),
        scratch_shapes=[pltpu.VMEM((tb + SUBLANES, cw), F32)],
        compiler_params=pltpu.CompilerParams(
            dimension_semantics=("arbitrary", "arbitrary"),
            vmem_limit_bytes=VMEM_LIMIT_BYTES),
        name="mixer",
    )(x, g1, w_in, conv_w, conv_p, kt, v, attn_p)


def _s5_kernel(u_ref, ere_ref, eim_ref, lcre_ref, lcim_ref, fre_ref, fim_ref, tz_ref,
               gw_ref, gb_ref, sp_ref, y_ref,
               utm_ref, x_ref, st_ref, hst_ref, yc_ref, ytm_ref, *, n_state):
    batch, n_t, sw = u_ref.shape
    n_c = n_t // SSM_CHUNK
    n_q = sw // QUAD_CH
    tile = SSM_CHUNK * QUAD_CH
    half = LANES // 2
    lo = lax.broadcasted_iota(jnp.int32, (batch, LANES), 1) < half
    step_i = pl.program_id(0)
    slot_a = step_i % 2
    slot_b = 1 - slot_a

    @pl.when(step_i == 0)
    def _():
        hst_ref[...] = jnp.zeros_like(hst_ref)
        ytm_ref[1] = jnp.zeros(ytm_ref.shape[1:], F32)

    sub_t = POST_ROWS // batch

    def post(sb):
        y = ytm_ref[slot_b, sb * POST_ROWS:(sb + 1) * POST_ROWS, :].reshape(sub_t, batch, sw)
        y = jnp.swapaxes(y, 0, 1).reshape(POST_ROWS, sw)
        y = y + y * jnp.tanh(_dot(y.astype(BF16), gw_ref[...]) + gb_ref[...])
        out = _dot(y.astype(BF16), sp_ref[...]).astype(BF16)
        y_ref[:, sb * sub_t:(sb + 1) * sub_t, :] = out.reshape(batch, sub_t, -1)

    assert n_t // sub_t == 4, "stage B is interleaved as four sub-blocks"

    n_pairs = n_q // 2

    def gather(j):
        for c in range(n_c):
            r0 = c * batch
            for kp in range(SSM_CHUNK // 2):
                t0 = (c * SSM_CHUNK + 2 * kp) * batch
                s0 = utm_ref[t0:t0 + batch, j * LANES:(j + 1) * LANES]
                s1 = utm_ref[t0 + batch:t0 + 2 * batch, j * LANES:(j + 1) * LANES]
                col = (2 * j) * tile + kp * LANES
                x_ref[r0:r0 + batch, col:col + LANES] = jnp.where(lo, s0, pltpu.roll(s1, half, 1))
                col += tile
                x_ref[r0:r0 + batch, col:col + LANES] = jnp.where(lo, pltpu.roll(s0, half, 1), s1)

    def chunk_inputs(j):
        for q in (2 * j, 2 * j + 1):
            xq = x_ref[:, q * tile:(q + 1) * tile].astype(BF16)
            st_ref[:, q * tile:(q + 1) * tile] = _dot(xq, ere_ref[q])
            st_ref[:, n_state + q * tile:n_state + (q + 1) * tile] = _dot(xq, eim_ref[q])

    def recurrence(j):
        re = slice(j * 2 * tile, (j + 1) * 2 * tile)
        im = slice(n_state + j * 2 * tile, n_state + (j + 1) * 2 * tile)
        l_re = lcre_ref[:, re]
        l_im = lcim_ref[:, re]
        h_re = hst_ref[:, re]
        h_im = hst_ref[:, im]
        for c in range(n_c):
            rows = slice(c * batch, (c + 1) * batch)
            b_re = st_ref[rows, re]
            b_im = st_ref[rows, im]
            st_ref[rows, re] = h_re
            st_ref[rows, im] = h_im
            h_re, h_im = (l_re * h_re - l_im * h_im + b_re, l_re * h_im + l_im * h_re + b_im)
        hst_ref[:, re] = h_re
        hst_ref[:, im] = h_im

    def chunk_outputs(j):
        for q in (2 * j, 2 * j + 1):
            cols = slice(q * tile, (q + 1) * tile)
            xq = x_ref[:, cols].astype(BF16)
            h_re = st_ref[:, cols].astype(BF16)
            h_im = st_ref[:, n_state + q * tile:n_state + (q + 1) * tile].astype(BF16)
            yc_ref[:, cols] = (_dot(h_re, fre_ref[q]) + _dot(h_im, fim_ref[q])
                               + _dot(xq, tz_ref[q]))

    def scatter(j):
        for c in range(n_c):
            r0 = c * batch
            for rp in range(SSM_CHUNK // 2):
                col = (2 * j) * tile + rp * LANES
                a = _gelu_tanh(yc_ref[r0:r0 + batch, col:col + LANES])
                b = _gelu_tanh(yc_ref[r0:r0 + batch, col + tile:col + tile + LANES])
                t0 = (c * SSM_CHUNK + 2 * rp) * batch
                ytm_ref[slot_a, t0:t0 + batch, j * LANES:(j + 1) * LANES] = jnp.where(
                    lo, a, pltpu.roll(b, half, 1))
                ytm_ref[slot_a, t0 + batch:t0 + 2 * batch, j * LANES:(j + 1) * LANES] = jnp.where(
                    lo, pltpu.roll(a, half, 1), b)

    phases = (gather, chunk_inputs, recurrence, chunk_outputs, scatter)
    assert n_pairs == 4 and n_state == n_q * tile, "the schedule below is written for 4 quad pairs"
    utm_ref[...] = jnp.swapaxes(u_ref[...].astype(F32), 0, 1).reshape(n_t * batch, sw)
    for tick in range(n_pairs + len(phases) - 1):
        for depth, phase in enumerate(phases):
            j = tick - depth
            if 0 <= j < n_pairs:
                phase(j)
        if tick % 2 == 1:
            post(tick // 2)


def _s5(u, ere, eim, lcre, lcim, fre, fim, tz, glu_w, glu_b, ssm_p):
    batch, s, sw = u.shape
    d = ssm_p.shape[1]
    n_state = lcre.shape[1]
    n_t = TIME_TILE
    n_blocks = s // n_t
    rows_c = (n_t // SSM_CHUNK) * batch
    kern = functools.partial(_s5_kernel, n_state=n_state)
    consts = (ere, eim, lcre, lcim, fre, fim, tz, glu_w, glu_b, ssm_p)
    return pl.pallas_call(
        kern,
        grid=(n_blocks + 1,),
        in_specs=[pl.BlockSpec((batch, n_t, sw), lambda i: (0, jnp.minimum(i, n_blocks - 1), 0))]
                 + [_const_spec(c.shape) for c in consts],
        out_specs=pl.BlockSpec((batch, n_t, d), lambda i: (0, jnp.maximum(i - 1, 0), 0)),
        out_shape=jax.ShapeDtypeStruct((batch, s, d), BF16),
        scratch_shapes=[pltpu.VMEM((batch * n_t, sw), F32),
                        pltpu.VMEM((rows_c, SSM_CHUNK * sw), F32),
                        pltpu.VMEM((rows_c, 2 * n_state), F32),
                        pltpu.VMEM((batch, 2 * n_state), F32),
                        pltpu.VMEM((rows_c, SSM_CHUNK * sw), F32),
                        pltpu.VMEM((2, batch * n_t, sw), F32)],
        compiler_params=pltpu.CompilerParams(
            dimension_semantics=("arbitrary",),
            vmem_limit_bytes=VMEM_LIMIT_BYTES),
        name="s5",
    )(u, *consts)


def _ffn_kernel(x_ref, part_ref, sgs_ref, yssm_ref, wo_ref, g2_ref, wg_ref, wu_ref, wd_ref,
                gf_ref, o_ref, *, chunks):
    tb = x_ref.shape[0] // FFN_SUBTILES
    subs = [slice(sub * tb, (sub + 1) * tb) for sub in range(FFN_SUBTILES)]
    x1s = []
    for rows in subs:
        merged = (part_ref[rows, :].astype(F32)
                  + sgs_ref[rows, :].astype(F32) * yssm_ref[rows, :].astype(F32))
        x1s.append(x_ref[rows, :] + _dot(merged.astype(BF16), wo_ref[...]))
    h2s = [_rms(x1, g2_ref[...]).astype(BF16) for x1 in x1s]
    outs = []
    for x1, h2 in zip(x1s, h2s):
        acc = x1
        for lo, hi in chunks:
            g = _dot(h2, wg_ref[:, lo:hi])
            a = ((g + g * jnp.tanh(g)) * _dot(h2, wu_ref[:, lo:hi])).astype(BF16)
            acc = acc + _dot(a, wd_ref[lo:hi, :])
        outs.append(_rms(acc, gf_ref[...]))
    o_ref[...] = jnp.concatenate(outs, axis=0)


def _ffn_chunks(hidden):
    step = 4 * MXU_TILE
    return tuple((lo, min(lo + step, hidden)) for lo in range(0, hidden, step))


def _ffn(x, part, sgs, yssm, w_o, g2, wg, wu, wd, gf):
    b, s, d = x.shape
    tb = FFN_ROW_TILE
    kern = functools.partial(_ffn_kernel, chunks=_ffn_chunks(wg.shape[1]))
    row_spec = pl.BlockSpec((None, tb, d), lambda bi, i: (bi, i, 0))
    return pl.pallas_call(
        kern,
        grid=(b, s // tb),
        in_specs=[row_spec, row_spec, row_spec, row_spec,
                  _const_spec(w_o.shape), _const_spec((1, d)),
                  _const_spec(wg.shape), _const_spec(wu.shape), _const_spec(wd.shape),
                  _const_spec((1, d))],
        out_specs=row_spec,
        out_shape=jax.ShapeDtypeStruct((b, s, d), x.dtype),
        compiler_params=pltpu.CompilerParams(
            dimension_semantics=("arbitrary", "arbitrary"),
            vmem_limit_bytes=FFN_VMEM_LIMIT_BYTES),
        name="ffn",
    )(x, part, sgs, yssm, w_o, g2, wg, wu, wd, gf)


def _pack_toeplitz(kq, d_skip, h):
    n, n_q, qc, _ = kq.shape
    idx = jnp.arange(qc)
    same_group = (idx[:, None] // h == idx[None, :] // h).astype(F32)
    blk = kq * same_group
    blk = blk.at[0].add(jnp.eye(qc, dtype=F32) * d_skip.reshape(n_q, 1, qc))
    zero = jnp.zeros_like(blk[0])
    rows = [jnp.concatenate([blk[r - k] if r >= k else zero for r in range(n)], axis=-1)
            for k in range(n)]
    return jnp.concatenate(rows, axis=1)


def kernel(x, mem, norm1_g, w_in, conv_w, conv_proj, ssm_A_re, ssm_A_im, ssm_log_dt, ssm_B_re,
           ssm_B_im, ssm_C_re, ssm_C_im, ssm_D, ssm_glu_w, ssm_glu_b, ssm_proj, mem_norm_g,
           attn_wk, attn_wv, attn_proj, w_o, norm2_g, ffn_w_gate, ffn_w_up, ffn_w_down,
           final_norm_g):
    b, s, d = x.shape
    depth = w_in.shape[0]
    cw = conv_w.shape[2]
    sw = ssm_D.shape[1]
    mw = attn_wk.shape[2]
    n_groups = ssm_A_re.shape[1]
    n_state = n_groups * SSM_STATE
    assert depth == 1, "the final norm is fused into the single layer's ffn kernel"
    assert b == SUBLANES, "the S5 recurrence keeps the batch on the sublane axis"

    for l in range(depth):
        lc_re, lc_im, e_re, e_im, f_re, f_im, kq = _ssm_prep(
            ssm_A_re[l], ssm_A_im[l], ssm_log_dt[l],
            ssm_B_re[l].transpose(0, 2, 1), ssm_B_im[l].transpose(0, 2, 1),
            ssm_C_re[l], ssm_C_im[l])
        ere = e_re.astype(BF16)
        eim = e_im.astype(BF16)
        fre = f_re.transpose(0, 2, 1).astype(BF16)
        fim = f_im.transpose(0, 2, 1).astype(BF16)
        tz = _pack_toeplitz(kq, ssm_D[l], SSM_GROUP).astype(BF16)
        lcre = jnp.broadcast_to(lc_re.reshape(1, n_state), (b, n_state))
        lcim = jnp.broadcast_to(lc_im.reshape(1, n_state), (b, n_state))

        kt, v = _mem_kv(mem, mem_norm_g[l].reshape(1, d), attn_wk[l].T.astype(BF16),
                        attn_wv[l].astype(BF16))
        n_gate = 3 * d
        col_scale = jnp.concatenate([jnp.ones((w_in.shape[2] - n_gate,), F32),
                                     jnp.full((n_gate,), 0.5, F32)])
        u, sgs, part = _mixer(
            x, norm1_g[l].reshape(1, d), (w_in[l] * col_scale).astype(BF16), conv_w[l],
            (0.5 * conv_proj[l]).astype(BF16), kt, v, (0.5 * attn_proj[l]).astype(BF16),
            cw=cw, sw=sw, mw=mw)
        yssm = _s5(u, ere, eim, lcre, lcim, fre, fim, tz,
                   (0.5 * ssm_glu_w[l]).astype(BF16), (0.5 * ssm_glu_b[l]).reshape(1, sw),
                   (0.5 * ssm_proj[l]).astype(BF16))
        x = _ffn(x, part, sgs, yssm, w_o[l].astype(BF16),
                 norm2_g[l].reshape(1, d), (0.5 * ffn_w_gate[l]).astype(BF16),
                 ffn_w_up[l].astype(BF16), ffn_w_down[l].astype(BF16), final_norm_g.reshape(1, d))
    return x
```

```python
import functools
import math

import jax
import jax.numpy as jnp
from jax import lax
from jax.experimental import pallas as pl
from jax.experimental.pallas import tpu as pltpu

F32 = jnp.float32
BF16 = jnp.bfloat16

EPS = 1e-6
SSM_GROUP = 16
SSM_STATE = 64
MEM_HEADS = 4
MEM_HEAD_DIM = 128

MXU_TILE = 256
SUBLANES = 8
LANES = 128
VMEM_LIMIT_BYTES = 56 * 1024 * 1024
FFN_VMEM_LIMIT_BYTES = 62 * 1024 * 1024

MIXER_ROW_TILE = 1024
FFN_ROW_TILE = 1024
FFN_SUBTILES = 4
TIME_TILE = 256
SSM_CHUNK = 4
QUAD_CH = MXU_TILE // SSM_CHUNK
POST_ROWS = 512


def _rms(x, g):
    ms = jnp.mean(x * x, axis=-1, keepdims=True)
    return x * lax.rsqrt(ms + EPS) * g


def _dot(a, b):
    return jnp.dot(a, b, preferred_element_type=F32)


def _gelu_tanh_x2(x):
    k0 = math.sqrt(2.0 / math.pi)
    return x + x * jnp.tanh(x * (k0 + (k0 * 0.044715) * (x * x)))


def _const_spec(shape):
    nd = len(shape)
    return pl.BlockSpec(shape, lambda *_: (0,) * nd, pipeline_mode=pl.Buffered(1))


def _ssm_prep_kernel(are_ref, aim_ref, ldt_ref, bre_ref, bim_ref, cre_ref, cim_ref,
                     lcre_ref, lcim_ref, ere_ref, eim_ref, fre_ref, fim_ref, kq_ref):
    a_re = are_ref[...]
    a_im = aim_ref[...]
    dt = jnp.exp(ldt_ref[...])
    mag = jnp.exp(a_re * dt)
    l_re = mag * jnp.cos(a_im * dt)
    l_im = mag * jnp.sin(a_im * dt)
    n_re = l_re - 1.0
    n_im = l_im
    den = a_re * a_re + a_im * a_im
    q_re = ((n_re * a_re + n_im * a_im) / den)[:, None, :]
    q_im = ((n_im * a_re - n_re * a_im) / den)[:, None, :]
    b_re = bre_ref[...]
    b_im = bim_ref[...]
    bb_re = q_re * b_re - q_im * b_im
    bb_im = q_re * b_im + q_im * b_re
    c_re = cre_ref[...]
    c_im = cim_ref[...]
    g, h, p = c_re.shape

    pows = [(jnp.ones_like(l_re), jnp.zeros_like(l_re))]
    for _ in range(SSM_CHUNK):
        pr, pi = pows[-1]
        pows.append((pr * l_re - pi * l_im, pr * l_im + pi * l_re))
    lcre_ref[...] = pows[SSM_CHUNK][0]
    lcim_ref[...] = pows[SSM_CHUNK][1]

    def place(out_ref, blocks):
        per_quad = QUAD_CH // h
        out_ref[...] = jnp.zeros(out_ref.shape, F32)
        for k, blk in enumerate(blocks):
            for gi in range(g):
                a = gi % per_quad
                r0 = (k * per_quad + a) * h
                out_ref[gi // per_quad, r0:r0 + h, a * p:(a + 1) * p] = blk[gi]

    e_re, e_im, f_re, f_im = [], [], [], []
    for k in range(SSM_CHUNK):
        pr, pi = pows[SSM_CHUNK - 1 - k]
        pr, pi = pr[:, None, :], pi[:, None, :]
        e_re.append(pr * bb_re - pi * bb_im)
        e_im.append(pr * bb_im + pi * bb_re)
    for r in range(SSM_CHUNK):
        pr, pi = pows[r + 1]
        pr, pi = pr[:, None, :], pi[:, None, :]
        f_re.append(c_re * pr - c_im * pi)
        f_im.append(-(c_re * pi + c_im * pr))
    place(ere_ref, e_re)
    place(eim_ref, e_im)
    place(fre_ref, f_re)
    place(fim_ref, f_im)
    nt = (((1,), (1,)), ((), ()))
    c_re2 = c_re.reshape(g * h, p)
    c_im2 = c_im.reshape(g * h, p)
    for j in range(SSM_CHUNK):
        pr, pi = pows[j]
        pr, pi = pr[:, None, :], pi[:, None, :]
        w_re = (pr * bb_re - pi * bb_im).reshape(g * h, p)
        w_im = (pr * bb_im + pi * bb_re).reshape(g * h, p)
        kk = (lax.dot_general(w_re, c_re2, nt, precision=lax.Precision.HIGHEST,
                              preferred_element_type=F32)
              - lax.dot_general(w_im, c_im2, nt, precision=lax.Precision.HIGHEST,
                                preferred_element_type=F32))
        for q in range(g * h // QUAD_CH):
            kq_ref[j, q] = kk[q * QUAD_CH:(q + 1) * QUAD_CH, q * QUAD_CH:(q + 1) * QUAD_CH]


def _ssm_prep(a_re, a_im, log_dt, b_re_t, b_im_t, c_re, c_im):
    g, p = a_re.shape
    h = b_re_t.shape[1]
    n = SSM_CHUNK
    return pl.pallas_call(
        _ssm_prep_kernel,
        out_shape=(jax.ShapeDtypeStruct((g, p), F32), jax.ShapeDtypeStruct((g, p), F32),
                   *[jax.ShapeDtypeStruct((g * h // QUAD_CH, n * QUAD_CH, n * QUAD_CH), F32)] * 4,
                   jax.ShapeDtypeStruct((n, g * h // QUAD_CH, QUAD_CH, QUAD_CH), F32)),
        name="ssm_prep",
    )(a_re, a_im, log_dt.reshape(g, 1), b_re_t, b_im_t, c_re, c_im)


def _mem_kv_kernel(mem_ref, g_ref, wkt_ref, wv_ref, kt_ref, v_ref):
    mn = _rms(mem_ref[...], g_ref[...]).astype(BF16)
    kt = lax.dot_general(wkt_ref[...], mn, (((1,), (1,)), ((), ())),
                         preferred_element_type=F32)
    kt_ref[...] = (kt * (MEM_HEAD_DIM ** -0.5 * math.log2(math.e))).astype(BF16)
    v_ref[...] = _dot(mn, wv_ref[...]).astype(BF16)


def _mem_kv(mem, g, wkt, wv):
    b, m, d = mem.shape
    w = wv.shape[1]
    return pl.pallas_call(
        _mem_kv_kernel,
        grid=(b,),
        in_specs=[pl.BlockSpec((None, m, d), lambda i: (i, 0, 0)),
                  pl.BlockSpec((1, d), lambda i: (0, 0)),
                  pl.BlockSpec((w, d), lambda i: (0, 0)),
                  pl.BlockSpec((d, w), lambda i: (0, 0))],
        out_specs=(pl.BlockSpec((None, w, m), lambda i: (i, 0, 0)),
                   pl.BlockSpec((None, m, w), lambda i: (i, 0, 0))),
        out_shape=(jax.ShapeDtypeStruct((b, w, m), BF16), jax.ShapeDtypeStruct((b, m, w), BF16)),
        name="mem_kv",
    )(mem, g, wkt, wv)


def _mixer_kernel(x_ref, g1_ref, win_ref, convw_ref, convp_ref, kt_ref, v_ref, attnp_ref,
                  u_ref, sgs_ref, part_ref, cv_ref, *, cw, sw, mw, d):
    i = pl.program_id(1)
    tb = x_ref.shape[0]
    h = _rms(x_ref[...], g1_ref[...]).astype(BF16)

    def proj(lo, hi):
        return _dot(h, win_ref[:, lo:hi])

    o_u = 3 * cw
    o_q = o_u + sw
    o_gc = o_q + mw
    o_gs = o_gc + d
    o_gm = o_gs + d

    q = proj(o_q, o_gc)
    scores = []
    for hd in range(MEM_HEADS):
        lo = hd * MEM_HEAD_DIM
        scores.append(_dot(q[:, lo:lo + MEM_HEAD_DIM].astype(BF16),
                           kt_ref[lo:lo + MEM_HEAD_DIM, :]))
    c_gate = proj(cw, 2 * cw)
    v_conv = proj(2 * cw, 3 * cw)

    heads = []
    for hd in range(MEM_HEADS):
        lo = hd * MEM_HEAD_DIM
        s = scores[hd]
        e = jnp.exp2(s - jnp.max(s, axis=-1, keepdims=True))
        pv = _dot(e.astype(BF16), v_ref[:, lo:lo + MEM_HEAD_DIM])
        heads.append(pv / jnp.sum(e, axis=-1, keepdims=True))
    o = jnp.concatenate(heads, axis=-1).astype(BF16)
    b_gate = proj(0, cw)

    cv = c_gate * v_conv

    @pl.when(i == 0)
    def _():
        cv_ref[0:SUBLANES, :] = jnp.zeros((SUBLANES, cw), F32)

    cv_ref[SUBLANES:SUBLANES + tb, :] = cv
    conv = (convw_ref[0:1, :] * cv_ref[SUBLANES - 2:SUBLANES - 2 + tb, :]
            + convw_ref[1:2, :] * cv_ref[SUBLANES - 1:SUBLANES - 1 + tb, :]
            + convw_ref[2:3, :] * cv)
    cv_ref[0:SUBLANES, :] = cv_ref[tb:tb + SUBLANES, :]
    pre = (b_gate * conv).astype(BF16)

    t_c = jnp.tanh(proj(o_gc, o_gs))
    y_mem = _dot(o, attnp_ref[...])
    y_conv = _dot(pre, convp_ref[...])
    t_m = jnp.tanh(proj(o_gm, o_gm + d))
    part_ref[...] = ((y_conv + y_mem) + (t_c * y_conv + t_m * y_mem)).astype(BF16)
    sgs_ref[...] = (0.5 + 0.5 * jnp.tanh(proj(o_gs, o_gm))).astype(BF16)
    u_ref[...] = proj(o_u, o_q).astype(BF16)


def _mixer(x, g1, w_in, conv_w, conv_p, kt, v, attn_p, *, cw, sw, mw):
    b, s, d = x.shape
    tb = MIXER_ROW_TILE
    m = kt.shape[2]
    kern = functools.partial(_mixer_kernel, cw=cw, sw=sw, mw=mw, d=d)
    return pl.pallas_call(
        kern,
        grid=(b, s // tb),
        in_specs=[pl.BlockSpec((None, tb, d), lambda bi, i: (bi, i, 0)),
                  _const_spec((1, d)),
                  _const_spec(w_in.shape),
                  _const_spec(conv_w.shape),
                  _const_spec(conv_p.shape),
                  pl.BlockSpec((None, mw, m), lambda bi, i: (bi, 0, 0)),
                  pl.BlockSpec((None, m, mw), lambda bi, i: (bi, 0, 0)),
                  _const_spec(attn_p.shape)],
        out_specs=(pl.BlockSpec((None, tb, sw), lambda bi, i: (bi, i, 0)),
                   pl.BlockSpec((None, tb, d), lambda bi, i: (bi, i, 0)),
                   pl.BlockSpec((None, tb, d), lambda bi, i: (bi, i, 0))),
        out_shape=(jax.ShapeDtypeStruct((b, s, sw), BF16),
                   jax.ShapeDtypeStruct((b, s, d), BF16),
                   jax.ShapeDtypeStruct((b, s, d), BF16)),
        scratch_shapes=[pltpu.VMEM((tb + SUBLANES, cw), F32)],
        compiler_params=pltpu.CompilerParams(
            dimension_semantics=("arbitrary", "arbitrary"),
            vmem_limit_bytes=VMEM_LIMIT_BYTES),
        name="mixer",
    )(x, g1, w_in, conv_w, conv_p, kt, v, attn_p)


def _s5_kernel(u_ref, ere_ref, eim_ref, lcre_ref, lcim_ref, fre_ref, fim_ref, tz_ref,
               gw_ref, gb_ref, sp_ref, y_ref,
               utm_ref, x_ref, st_ref, hst_ref, yc_ref, ytm_ref, *, n_state):
    batch, n_t, sw = u_ref.shape
    n_c = n_t // SSM_CHUNK
    n_q = sw // QUAD_CH
    tile = SSM_CHUNK * QUAD_CH
    half = LANES // 2
    lo = lax.broadcasted_iota(jnp.int32, (batch, LANES), 1) < half
    step_i = pl.program_id(0)
    slot_a = step_i % 2
    slot_b = 1 - slot_a

    @pl.when(step_i == 0)
    def _():
        hst_ref[...] = jnp.zeros_like(hst_ref)
        ytm_ref[1] = jnp.zeros(ytm_ref.shape[1:], F32)

    sub_t = POST_ROWS // batch

    def post(sb):
        y = ytm_ref[slot_b, sb * POST_ROWS:(sb + 1) * POST_ROWS, :].reshape(sub_t, batch, sw)
        y = jnp.swapaxes(y, 0, 1).reshape(POST_ROWS, sw)
        y = y + y * jnp.tanh(_dot(y.astype(BF16), gw_ref[...]) + gb_ref[...])
        out = _dot(y.astype(BF16), sp_ref[...]).astype(BF16)
        y_ref[:, sb * sub_t:(sb + 1) * sub_t, :] = out.reshape(batch, sub_t, -1)

    assert n_t // sub_t == 4, "stage B is interleaved as four sub-blocks"

    n_pairs = n_q // 2

    def gather(j):
        for c in range(n_c):
            r0 = c * batch
            for kp in range(SSM_CHUNK // 2):
                t0 = (c * SSM_CHUNK + 2 * kp) * batch
                s0 = utm_ref[t0:t0 + batch, j * LANES:(j + 1) * LANES]
                s1 = utm_ref[t0 + batch:t0 + 2 * batch, j * LANES:(j + 1) * LANES]
                col = (2 * j) * tile + kp * LANES
                x_ref[r0:r0 + batch, col:col + LANES] = jnp.where(lo, s0, pltpu.roll(s1, half, 1))
                col += tile
                x_ref[r0:r0 + batch, col:col + LANES] = jnp.where(lo, pltpu.roll(s0, half, 1), s1)

    def chunk_inputs(j):
        for q in (2 * j, 2 * j + 1):
            xq = x_ref[:, q * tile:(q + 1) * tile].astype(BF16)
            st_ref[:, q * tile:(q + 1) * tile] = _dot(xq, ere_ref[q])
            st_ref[:, n_state + q * tile:n_state + (q + 1) * tile] = _dot(xq, eim_ref[q])

    def recurrence(j):
        re = slice(j * 2 * tile, (j + 1) * 2 * tile)
        im = slice(n_state + j * 2 * tile, n_state + (j + 1) * 2 * tile)
        l_re = lcre_ref[:, re]
        l_im = lcim_ref[:, re]
        h_re = hst_ref[:, re]
        h_im = hst_ref[:, im]
        for c in range(n_c):
            rows = slice(c * batch, (c + 1) * batch)
            b_re = st_ref[rows, re]
            b_im = st_ref[rows, im]
            st_ref[rows, re] = h_re
            st_ref[rows, im] = h_im
            h_re, h_im = (l_re * h_re - l_im * h_im + b_re, l_re * h_im + l_im * h_re + b_im)
        hst_ref[:, re] = h_re
        hst_ref[:, im] = h_im

    def chunk_outputs(j):
        for q in (2 * j, 2 * j + 1):
            cols = slice(q * tile, (q + 1) * tile)
            xq = x_ref[:, cols].astype(BF16)
            h_re = st_ref[:, cols].astype(BF16)
            h_im = st_ref[:, n_state + q * tile:n_state + (q + 1) * tile].astype(BF16)
            yc_ref[:, cols] = (_dot(h_re, fre_ref[q]) + _dot(h_im, fim_ref[q])
                               + _dot(xq, tz_ref[q]))

    def scatter(j):
        for c in range(n_c):
            r0 = c * batch
            for rp in range(SSM_CHUNK // 2):
                col = (2 * j) * tile + rp * LANES
                a = _gelu_tanh_x2(yc_ref[r0:r0 + batch, col:col + LANES])
                b = _gelu_tanh_x2(yc_ref[r0:r0 + batch, col + tile:col + tile + LANES])
                t0 = (c * SSM_CHUNK + 2 * rp) * batch
                ytm_ref[slot_a, t0:t0 + batch, j * LANES:(j + 1) * LANES] = jnp.where(
                    lo, a, pltpu.roll(b, half, 1))
                ytm_ref[slot_a, t0 + batch:t0 + 2 * batch, j * LANES:(j + 1) * LANES] = jnp.where(
                    lo, pltpu.roll(a, half, 1), b)

    phases = (gather, chunk_inputs, recurrence, chunk_outputs, scatter)
    assert n_pairs == 4 and n_state == n_q * tile, "the schedule below is written for 4 quad pairs"
    utm_ref[...] = jnp.swapaxes(u_ref[...].astype(F32), 0, 1).reshape(n_t * batch, sw)
    for tick in range(n_pairs + len(phases) - 1):
        for depth, phase in enumerate(phases):
            j = tick - depth
            if 0 <= j < n_pairs:
                phase(j)
        if tick % 2 == 1:
            post(tick // 2)


def _s5(u, ere, eim, lcre, lcim, fre, fim, tz, glu_w, glu_b, ssm_p):
    batch, s, sw = u.shape
    d = ssm_p.shape[1]
    n_state = lcre.shape[1]
    n_t = TIME_TILE
    n_blocks = s // n_t
    rows_c = (n_t // SSM_CHUNK) * batch
    kern = functools.partial(_s5_kernel, n_state=n_state)
    consts = (ere, eim, lcre, lcim, fre, fim, tz, glu_w, glu_b, ssm_p)
    return pl.pallas_call(
        kern,
        grid=(n_blocks + 1,),
        in_specs=[pl.BlockSpec((batch, n_t, sw), lambda i: (0, jnp.minimum(i, n_blocks - 1), 0))]
                 + [_const_spec(c.shape) for c in consts],
        out_specs=pl.BlockSpec((batch, n_t, d), lambda i: (0, jnp.maximum(i - 1, 0), 0)),
        out_shape=jax.ShapeDtypeStruct((batch, s, d), BF16),
        scratch_shapes=[pltpu.VMEM((batch * n_t, sw), F32),
                        pltpu.VMEM((rows_c, SSM_CHUNK * sw), F32),
                        pltpu.VMEM((rows_c, 2 * n_state), F32),
                        pltpu.VMEM((batch, 2 * n_state), F32),
                        pltpu.VMEM((rows_c, SSM_CHUNK * sw), F32),
                        pltpu.VMEM((2, batch * n_t, sw), F32)],
        compiler_params=pltpu.CompilerParams(
            dimension_semantics=("arbitrary",),
            vmem_limit_bytes=VMEM_LIMIT_BYTES),
        name="s5",
    )(u, *consts)


def _ffn_kernel(x_ref, part_ref, sgs_ref, yssm_ref, wo_ref, g2_ref, wg_ref, wu_ref, wd_ref,
                gf_ref, o_ref, *, chunks):
    tb = x_ref.shape[0] // FFN_SUBTILES
    subs = [slice(sub * tb, (sub + 1) * tb) for sub in range(FFN_SUBTILES)]
    x1s = []
    for rows in subs:
        merged = (part_ref[rows, :].astype(F32)
                  + sgs_ref[rows, :].astype(F32) * yssm_ref[rows, :].astype(F32))
        x1s.append(x_ref[rows, :] + _dot(merged.astype(BF16), wo_ref[...]))
    h2s = [_rms(x1, g2_ref[...]).astype(BF16) for x1 in x1s]
    outs = []
    for x1, h2 in zip(x1s, h2s):
        acc = x1
        for lo, hi in chunks:
            g = _dot(h2, wg_ref[:, lo:hi])
            a = ((g + g * jnp.tanh(g)) * _dot(h2, wu_ref[:, lo:hi])).astype(BF16)
            acc = acc + _dot(a, wd_ref[lo:hi, :])
        outs.append(_rms(acc, gf_ref[...]))
    o_ref[...] = jnp.concatenate(outs, axis=0)


def _ffn_chunks(hidden):
    step = 4 * MXU_TILE
    return tuple((lo, min(lo + step, hidden)) for lo in range(0, hidden, step))


def _ffn(x, part, sgs, yssm, w_o, g2, wg, wu, wd, gf):
    b, s, d = x.shape
    tb = FFN_ROW_TILE
    kern = functools.partial(_ffn_kernel, chunks=_ffn_chunks(wg.shape[1]))
    row_spec = pl.BlockSpec((None, tb, d), lambda bi, i: (bi, i, 0))
    return pl.pallas_call(
        kern,
        grid=(b, s // tb),
        in_specs=[row_spec, row_spec, row_spec, row_spec,
                  _const_spec(w_o.shape), _const_spec((1, d)),
                  _const_spec(wg.shape), _const_spec(wu.shape), _const_spec(wd.shape),
                  _const_spec((1, d))],
        out_specs=row_spec,
        out_shape=jax.ShapeDtypeStruct((b, s, d), x.dtype),
        compiler_params=pltpu.CompilerParams(
            dimension_semantics=("arbitrary", "arbitrary"),
            vmem_limit_bytes=FFN_VMEM_LIMIT_BYTES),
        name="ffn",
    )(x, part, sgs, yssm, w_o, g2, wg, wu, wd, gf)


def _pack_toeplitz(kq, d_skip, h):
    n, n_q, qc, _ = kq.shape
    idx = jnp.arange(qc)
    same_group = (idx[:, None] // h == idx[None, :] // h).astype(F32)
    blk = kq * same_group
    blk = blk.at[0].add(jnp.eye(qc, dtype=F32) * d_skip.reshape(n_q, 1, qc))
    zero = jnp.zeros_like(blk[0])
    rows = [jnp.concatenate([blk[r - k] if r >= k else zero for r in range(n)], axis=-1)
            for k in range(n)]
    return jnp.concatenate(rows, axis=1)


def kernel(x, mem, norm1_g, w_in, conv_w, conv_proj, ssm_A_re, ssm_A_im, ssm_log_dt, ssm_B_re,
           ssm_B_im, ssm_C_re, ssm_C_im, ssm_D, ssm_glu_w, ssm_glu_b, ssm_proj, mem_norm_g,
           attn_wk, attn_wv, attn_proj, w_o, norm2_g, ffn_w_gate, ffn_w_up, ffn_w_down,
           final_norm_g):
    b, s, d = x.shape
    depth = w_in.shape[0]
    cw = conv_w.shape[2]
    sw = ssm_D.shape[1]
    mw = attn_wk.shape[2]
    n_groups = ssm_A_re.shape[1]
    n_state = n_groups * SSM_STATE
    assert depth == 1, "the final norm is fused into the single layer's ffn kernel"
    assert b == SUBLANES, "the S5 recurrence keeps the batch on the sublane axis"

    for l in range(depth):
        lc_re, lc_im, e_re, e_im, f_re, f_im, kq = _ssm_prep(
            ssm_A_re[l], ssm_A_im[l], ssm_log_dt[l],
            ssm_B_re[l].transpose(0, 2, 1), ssm_B_im[l].transpose(0, 2, 1),
            ssm_C_re[l], ssm_C_im[l])
        ere = e_re.astype(BF16)
        eim = e_im.astype(BF16)
        fre = f_re.transpose(0, 2, 1).astype(BF16)
        fim = f_im.transpose(0, 2, 1).astype(BF16)
        tz = _pack_toeplitz(kq, ssm_D[l], SSM_GROUP).astype(BF16)
        lcre = jnp.broadcast_to(lc_re.reshape(1, n_state), (b, n_state))
        lcim = jnp.broadcast_to(lc_im.reshape(1, n_state), (b, n_state))

        kt, v = _mem_kv(mem, mem_norm_g[l].reshape(1, d), attn_wk[l].T.astype(BF16),
                        attn_wv[l].astype(BF16))
        n_gate = 3 * d
        col_scale = jnp.concatenate([jnp.ones((w_in.shape[2] - n_gate,), F32),
                                     jnp.full((n_gate,), 0.5, F32)])
        u, sgs, part = _mixer(
            x, norm1_g[l].reshape(1, d), (w_in[l] * col_scale).astype(BF16), conv_w[l],
            (0.5 * conv_proj[l]).astype(BF16), kt, v, (0.5 * attn_proj[l]).astype(BF16),
            cw=cw, sw=sw, mw=mw)
        yssm = _s5(u, ere, eim, lcre, lcim, fre, fim, tz,
                   (0.25 * ssm_glu_w[l]).astype(BF16), (0.5 * ssm_glu_b[l]).reshape(1, sw),
                   (0.25 * ssm_proj[l]).astype(BF16))
        x = _ffn(x, part, sgs, yssm, w_o[l].astype(BF16),
                 norm2_g[l].reshape(1, d), (0.5 * ffn_w_gate[l]).astype(BF16),
                 ffn_w_up[l].astype(BF16), ffn_w_down[l].astype(BF16), final_norm_g.reshape(1, d))
    return x
```
